```python
import math
import jax
import jax.numpy as jnp
from jax import lax
import numpy as np


D_MODEL = 2048
BATCH = 2
SEQ = 16384
DEPTH = 2

CTX_LEN = 256
GRID_W = 64
ROPE_BASE = 10000.0
EPS = 1e-6
Q_BLOCK = 128
HEAD_DIM = 128

MLA_HEADS = 4
MLA_Q_RANK = 512
MLA_KV_RANK = 256
MLA_NOPE = 128
MLA_ROPE = 64
MLA_V = 128

GQA_HEADS = 4
GQA_KV_HEADS = 2

SSD_HEADS = 8
SSD_HEAD_DIM = 64
SSD_INNER = SSD_HEADS * SSD_HEAD_DIM
SSD_GROUPS = 2
SSD_STATE = 128
SSD_CONV = 5
SSD_CHUNK = 128
SSD_CONV_DIM = SSD_INNER + 2 * SSD_GROUPS * SSD_STATE

SWA_HEADS = 4
SWA_KV_HEADS = 2
WINDOW = 128

PEER_HEADS = 8
PEER_KEYS = 128
PEER_EXPERTS = PEER_KEYS * PEER_KEYS
PEER_TOPK = 16
PEER_DKEY = 256
PEER_BLOCK = 128

MIX_WIDTH = MLA_HEADS * MLA_V + GQA_HEADS * HEAD_DIM + SSD_INNER + SWA_HEADS * HEAD_DIM
MLA_COLS = MLA_Q_RANK + MLA_KV_RANK + MLA_ROPE
GQA_COLS = (GQA_HEADS + 2 * GQA_KV_HEADS) * HEAD_DIM
SSD_COLS = SSD_INNER + SSD_CONV_DIM + 2 * SSD_HEADS
SWA_COLS = (SWA_HEADS + 2 * SWA_KV_HEADS) * HEAD_DIM
IN_COLS = MLA_COLS + GQA_COLS + SSD_COLS + SWA_COLS
GROUP_SPLITS = [MLA_COLS, MLA_COLS + GQA_COLS, MLA_COLS + GQA_COLS + SSD_COLS]

kernel_name = 'hybrid_flow_backbone_peer'


def rms_norm(x, g):
    xf = x.astype(jnp.float32)
    y = xf * lax.rsqrt(jnp.mean(xf * xf, axis=-1, keepdims=True) + EPS)
    return (y * g.astype(jnp.float32)).astype(x.dtype)


def layer_norm(x, g, b):
    xf = x.astype(jnp.float32)
    mu = jnp.mean(xf, axis=-1, keepdims=True)
    var = jnp.mean(jnp.square(xf - mu), axis=-1, keepdims=True)
    y = (xf - mu) * lax.rsqrt(var + EPS) * g.astype(jnp.float32) + b.astype(jnp.float32)
    return y.astype(x.dtype)


def modulate(x, shift, scale):
    return x * (1 + scale) + shift


def axial_rope_tables(rows, rot_dim):
    row = jnp.repeat(jnp.arange(rows, dtype=jnp.float32), GRID_W)
    col = jnp.tile(jnp.arange(GRID_W, dtype=jnp.float32), rows)
    n_freq = rot_dim // 4
    inv_freq = ROPE_BASE ** (-jnp.arange(n_freq, dtype=jnp.float32) / n_freq)
    ang = jnp.concatenate([row[:, None] * inv_freq, col[:, None] * inv_freq], axis=-1)
    return jnp.cos(ang), jnp.sin(ang)


def apply_rope(x, cos, sin):
    half = x.shape[-1] // 2
    x1, x2 = x[..., :half], x[..., half:]
    c = cos[None, :, None, :].astype(x.dtype)
    s = sin[None, :, None, :].astype(x.dtype)
    return jnp.concatenate([x1 * c - x2 * s, x1 * s + x2 * c], axis=-1)


def block_attention(q, k, v, scale):
    bsz, s_len, hk, g, dk = q.shape
    nb = s_len // Q_BLOCK
    qb = jnp.moveaxis(q.reshape(bsz, nb, Q_BLOCK, hk, g, dk), 1, 0)

    def one_block(qblk):
        s = jnp.einsum('bqhgd,bkhd->bhgqk', qblk, k).astype(jnp.float32) * scale
        p = jax.nn.softmax(s, axis=-1).astype(v.dtype)
        return jnp.einsum('bhgqk,bkhd->bqhgd', p, v)

    o = lax.map(one_block, qb)
    return jnp.moveaxis(o, 0, 1).reshape(bsz, s_len, hk * g * v.shape[-1])


def window_attention(q, k, v, kc, vc, sink, scale):
    bsz, s_len, hk, g, d = q.shape
    nb = s_len // WINDOW
    qb = q.reshape(bsz, nb, WINDOW, hk, g, d)
    pad = ((0, 0), (WINDOW, WINDOW), (0, 0), (0, 0))
    kp = jnp.pad(k, pad).reshape(bsz, nb + 2, WINDOW, hk, d)
    vp = jnp.pad(v, pad).reshape(bsz, nb + 2, WINDOW, hk, d)
    kband = jnp.concatenate([kp[:, :-2], kp[:, 1:-1], kp[:, 2:]], axis=2)
    vband = jnp.concatenate([vp[:, :-2], vp[:, 1:-1], vp[:, 2:]], axis=2)
    s_win = jnp.einsum('bnqhgd,bnkhd->bnhgqk', qb, kband).astype(jnp.float32) * scale
    blk = jnp.arange(nb)[:, None] * WINDOW
    q_pos = blk + jnp.arange(WINDOW)[None, :]
    k_pos = blk - WINDOW + jnp.arange(3 * WINDOW)[None, :]
    valid = (jnp.abs(k_pos[:, None, :] - q_pos[:, :, None]) <= WINDOW) & (k_pos >= 0)[:, None, :] & (k_pos < s_len)[:, None, :]
    s_win = jnp.where(valid[None, :, None, None], s_win, -jnp.inf)
    s_ctx = jnp.einsum('bnqhgd,bchd->bnhgqc', qb, kc).astype(jnp.float32) * scale
    s_sink = jnp.broadcast_to(sink.reshape(hk, g)[None, None, :, :, None, None].astype(jnp.float32), s_ctx.shape[:-1] + (1,))
    p = jax.nn.softmax(jnp.concatenate([s_ctx, s_win, s_sink], axis=-1), axis=-1).astype(v.dtype)
    n_ctx = kc.shape[1]
    o = jnp.einsum('bnhgqc,bchd->bnqhgd', p[..., :n_ctx], vc) + jnp.einsum('bnhgqk,bnkhd->bnqhgd', p[..., n_ctx:n_ctx + 3 * WINDOW], vband)
    return o.reshape(bsz, s_len, hk * g * d)


def sink_attention(q, k, v, sink, scale):
    bsz, t_len, hk, g, d = q.shape
    s = jnp.einsum('bqhgd,bkhd->bhgqk', q, k).astype(jnp.float32) * scale
    s_sink = jnp.broadcast_to(sink.reshape(hk, g)[None, :, :, None, None].astype(jnp.float32), s.shape[:-1] + (1,))
    p = jax.nn.softmax(jnp.concatenate([s, s_sink], axis=-1), axis=-1)[..., :-1].astype(v.dtype)
    return jnp.einsum('bhgqk,bkhd->bqhgd', p, v).reshape(bsz, t_len, hk * g * d)


def attn_heads(p, n_q, n_kv):
    bsz, t_len, _ = p.shape
    q, k, v = jnp.split(p, [n_q * HEAD_DIM, (n_q + n_kv) * HEAD_DIM], axis=-1)
    return (q.reshape(bsz, t_len, n_q, HEAD_DIM), k.reshape(bsz, t_len, n_kv, HEAD_DIM), v.reshape(bsz, t_len, n_kv, HEAD_DIM))


def group_q(q, n_kv):
    bsz, t_len, h, d = q.shape
    return q.reshape(bsz, t_len, n_kv, h // n_kv, d)


def mla_queries(p, q_norm, w_uq, rope):
    bsz, t_len, _ = p.shape
    q = (rms_norm(p[..., :MLA_Q_RANK], q_norm) @ w_uq).reshape(bsz, t_len, MLA_HEADS, MLA_NOPE + MLA_ROPE)
    if rope is not None:
        q = jnp.concatenate([q[..., :MLA_NOPE], apply_rope(q[..., MLA_NOPE:], *rope)], axis=-1)
    return q[:, :, :, None, :]


def mla_keys_values(p, kv_norm, w_ukv, rope):
    bsz, t_len, _ = p.shape
    ckv = p[..., MLA_Q_RANK:MLA_Q_RANK + MLA_KV_RANK]
    k_rope = p[..., MLA_Q_RANK + MLA_KV_RANK:][:, :, None, :]
    if rope is not None:
        k_rope = apply_rope(k_rope, *rope)
    kv = (rms_norm(ckv, kv_norm) @ w_ukv).reshape(bsz, t_len, MLA_HEADS, MLA_NOPE + MLA_V)
    k = jnp.concatenate([kv[..., :MLA_NOPE], jnp.broadcast_to(k_rope, (bsz, t_len, MLA_HEADS, MLA_ROPE))], axis=-1)
    return k, kv[..., MLA_NOPE:]


def mla_mixer(px, pc, q_norm, w_uq, kv_norm, w_ukv, rope, ctx_out):
    scale = (MLA_NOPE + MLA_ROPE) ** -0.5
    kx, vx = mla_keys_values(px, kv_norm, w_ukv, rope)
    kc, vc = mla_keys_values(pc, kv_norm, w_ukv, None)
    qx = mla_queries(px, q_norm, w_uq, rope)
    y_x = block_attention(qx, jnp.concatenate([kc, kx], axis=1), jnp.concatenate([vc, vx], axis=1), scale)
    y_c = block_attention(mla_queries(pc, q_norm, w_uq, None), kc, vc, scale) if ctx_out else None
    return y_x, y_c


def gqa_mixer(px, pc, q_norm, k_norm, rope, ctx_out):
    scale = HEAD_DIM ** -0.5
    qx, kx, vx = attn_heads(px, GQA_HEADS, GQA_KV_HEADS)
    qc, kc, vc = attn_heads(pc, GQA_HEADS, GQA_KV_HEADS)
    qx = apply_rope(rms_norm(qx, q_norm), *rope)
    kx = apply_rope(rms_norm(kx, k_norm), *rope)
    kc = rms_norm(kc, k_norm)
    y_x = block_attention(group_q(qx, GQA_KV_HEADS), jnp.concatenate([kc, kx], axis=1), jnp.concatenate([vc, vx], axis=1), scale)
    y_c = block_attention(group_q(rms_norm(qc, q_norm), GQA_KV_HEADS), kc, vc, scale) if ctx_out else None
    return y_x, y_c


def swa_mixer(px, pc, sink, rope, ctx_out):
    scale = HEAD_DIM ** -0.5
    qx, kx, vx = attn_heads(px, SWA_HEADS, SWA_KV_HEADS)
    qc, kc, vc = attn_heads(pc, SWA_HEADS, SWA_KV_HEADS)
    qx = apply_rope(qx, *rope)
    kx = apply_rope(kx, *rope)
    y_x = window_attention(group_q(qx, SWA_KV_HEADS), kx, vx, kc, vc, sink, scale)
    y_c = sink_attention(group_q(qc, SWA_KV_HEADS), kc, vc, sink, scale) if ctx_out else None
    return y_x, y_c


def centred_dwconv(x, w, b):
    y = lax.conv_general_dilated(x, w[:, None, :], window_strides=(1,), padding=((SSD_CONV // 2, SSD_CONV // 2),), dimension_numbers=('NWC', 'WIO', 'NWC'), feature_group_count=x.shape[-1])
    return y + b


def ssd_chunked(x, dt, A, Bm, Cm, h0, with_output):
    bsz, t_len = x.shape[:2]
    L, G, Hg, P, N = SSD_CHUNK, SSD_GROUPS, SSD_HEADS // SSD_GROUPS, SSD_HEAD_DIM, SSD_STATE
    nc = t_len // L
    f32 = jnp.float32
    xc = x.astype(f32).reshape(bsz, nc, L, G, Hg, P)
    dtc = dt.astype(f32).reshape(bsz, nc, L, G, Hg)
    Bc = Bm.astype(f32).reshape(bsz, nc, L, G, N)
    Cc = Cm.astype(f32).reshape(bsz, nc, L, G, N)
    a_cum = jnp.cumsum(dtc * A.astype(f32).reshape(G, Hg), axis=2)
    a_end = a_cum[:, :, -1]
    states = jnp.einsum('bclgn,bclgh,bclghp->bcghpn', Bc, jnp.exp(a_end[:, :, None] - a_cum) * dtc, xc)

    def carry_state(h, inp):
        decay, st = inp
        return h * decay[..., None, None] + st, h

    h_final, h_enter = lax.scan(carry_state, h0.astype(f32), (jnp.moveaxis(jnp.exp(a_end), 1, 0), jnp.moveaxis(states, 1, 0)))
    if not with_output:
        return None, h_final
    h_enter = jnp.moveaxis(h_enter, 0, 1)
    causal = jnp.tril(jnp.ones((L, L), dtype=bool))
    seg = a_cum[:, :, :, None] - a_cum[:, :, None]
    decay = jnp.exp(jnp.where(causal[:, :, None, None], seg, -jnp.inf))
    cb = jnp.einsum('bclgn,bcsgn->bclsg', Cc, Bc)
    y_diag = jnp.einsum('bclsgh,bcsghp->bclghp', cb[..., None] * decay * dtc[:, :, None], xc)
    y_off = jnp.einsum('bclgn,bcghpn->bclghp', Cc, h_enter) * jnp.exp(a_cum)[..., None]
    return (y_diag + y_off).reshape(bsz, t_len, SSD_HEADS, P), h_final


def ssd_mixer(px, pc, conv_w, conv_b, a_log, dt_bias, d_skip, norm_g, ctx_out):
    def prep(p):
        bsz, t_len, _ = p.shape
        z, xbc, dt = jnp.split(p, [SSD_INNER, SSD_INNER + SSD_CONV_DIM], axis=-1)
        xbc = jax.nn.silu(centred_dwconv(xbc, conv_w, conv_b))
        xs, Bm, Cm = jnp.split(xbc, [SSD_INNER, SSD_INNER + SSD_GROUPS * SSD_STATE], axis=-1)
        xs = xs.reshape(bsz, t_len, SSD_HEADS, SSD_HEAD_DIM)
        Bm = Bm.reshape(bsz, t_len, SSD_GROUPS, SSD_STATE)
        Cm = Cm.reshape(bsz, t_len, SSD_GROUPS, SSD_STATE)
        dt = jax.nn.softplus(dt.reshape(bsz, t_len, 2, SSD_HEADS) + dt_bias)
        return z, xs, Bm, Cm, dt

    def flip(a):
        return jnp.flip(a, axis=1)

    A = -jnp.exp(a_log.astype(jnp.float32))
    zx, xx, Bx, Cx, dtx = prep(px)
    zc, xc, Bc, Cc, dtc = prep(pc)
    h0 = jnp.zeros((px.shape[0], SSD_GROUPS, SSD_HEADS // SSD_GROUPS, SSD_HEAD_DIM, SSD_STATE), jnp.float32)
    yc_f, hc_f = ssd_chunked(xc, dtc[:, :, 0], A[0], Bc, Cc, h0, ctx_out)
    yx_f, _ = ssd_chunked(xx, dtx[:, :, 0], A[0], Bx, Cx, hc_f, True)
    yc_b, hc_b = ssd_chunked(flip(xc), flip(dtc[:, :, 1]), A[1], flip(Bc), flip(Cc), h0, ctx_out)
    yx_b, _ = ssd_chunked(flip(xx), flip(dtx[:, :, 1]), A[1], flip(Bx), flip(Cx), hc_b, True)

    def finish(y_f, y_b, xs, z):
        y = (y_f + flip(y_b) + d_skip.astype(jnp.float32)[:, None] * xs.astype(jnp.float32)).astype(xs.dtype)
        bsz, t_len = xs.shape[:2]
        return rms_norm(y.reshape(bsz, t_len, SSD_INNER) * jax.nn.silu(z), norm_g)

    y_x = finish(yx_f, yx_b, xx, zx)
    y_c = finish(yc_f, yc_b, xc, zc) if ctx_out else None
    return y_x, y_c


def peer_ffn(h, w_q, sub_keys, u, v):
    shape = h.shape
    hb = h.reshape(-1, PEER_BLOCK, shape[-1])

    def one_block(t):
        q = (t @ w_q).reshape(PEER_BLOCK, PEER_HEADS, 2, PEER_DKEY // 2)
        s = jnp.einsum('thpd,hpkd->thpk', q, sub_keys).astype(jnp.float32)
        s1, i1 = lax.top_k(s[:, :, 0], PEER_TOPK)
        s2, i2 = lax.top_k(s[:, :, 1], PEER_TOPK)
        cand = (s1[..., :, None] + s2[..., None, :]).reshape(PEER_BLOCK, PEER_HEADS, PEER_TOPK * PEER_TOPK)
        sc, ci = lax.top_k(cand, PEER_TOPK)
        e = jnp.take_along_axis(i1, ci // PEER_TOPK, axis=-1) * PEER_KEYS + jnp.take_along_axis(i2, ci % PEER_TOPK, axis=-1)
        g = jax.nn.softmax(sc, axis=-1).astype(t.dtype)
        act = jax.nn.gelu(jnp.einsum('thkd,td->thk', u[e], t))
        return jnp.einsum('thk,thkd->td', g * act, v[e])

    return lax.map(one_block, hb).reshape(shape)


def _normal(key, shape, std):
    return std * jax.random.normal(key, shape, jnp.float32)


def setup_inputs(seed: int = 0) -> dict:
    key = jax.random.key(seed)
    ks = iter(jax.random.split(key, 40))
    L, D = DEPTH, D_MODEL
    beta = (8.0 * DEPTH) ** -0.25
    dt0 = jnp.exp(jax.random.uniform(next(ks), (L, 2, SSD_HEADS), jnp.float32, math.log(1e-3), math.log(1e-1)))
    return {
        'x': _normal(next(ks), (BATCH, SEQ, D), 1.0),
        'c': _normal(next(ks), (BATCH, D), 1.0),
        'ctx': _normal(next(ks), (BATCH, CTX_LEN, D), 1.0),
        'c_ctx': _normal(next(ks), (D,), 1.0),
        'w_ada': _normal(next(ks), (L, D, 6 * D), D ** -0.5),
        'b_ada': _normal(next(ks), (L, 6 * D), 0.01),
        'w_in': _normal(next(ks), (L, D, IN_COLS), D ** -0.5),
        'mla_q_norm': 1.0 + _normal(next(ks), (L, MLA_Q_RANK), 0.01),
        'mla_w_uq': _normal(next(ks), (L, MLA_Q_RANK, MLA_HEADS * (MLA_NOPE + MLA_ROPE)), MLA_Q_RANK ** -0.5),
        'mla_kv_norm': 1.0 + _normal(next(ks), (L, MLA_KV_RANK), 0.01),
        'mla_w_ukv': _normal(next(ks), (L, MLA_KV_RANK, MLA_HEADS * (MLA_NOPE + MLA_V)), MLA_KV_RANK ** -0.5),
        'gqa_q_norm': 1.0 + _normal(next(ks), (L, HEAD_DIM), 0.01),
        'gqa_k_norm': 1.0 + _normal(next(ks), (L, HEAD_DIM), 0.01),
        'ssd_conv_w': _normal(next(ks), (L, SSD_CONV, SSD_CONV_DIM), SSD_CONV ** -0.5),
        'ssd_conv_b': _normal(next(ks), (L, SSD_CONV_DIM), 0.01),
        'ssd_a_log': jnp.log(jax.random.uniform(next(ks), (L, 2, SSD_HEADS), jnp.float32, 1.0, 16.0)),
        'ssd_dt_bias': dt0 + jnp.log(-jnp.expm1(-dt0)),
        'ssd_d': 1.0 + _normal(next(ks), (L, SSD_HEADS), 0.01),
        'ssd_norm': 1.0 + _normal(next(ks), (L, SSD_INNER), 0.01),
        'swa_sink': _normal(next(ks), (L, SWA_HEADS), 0.5),
        'w_out': _normal(next(ks), (L, MIX_WIDTH, D), beta * MIX_WIDTH ** -0.5),
        'ln1_g': 1.0 + _normal(next(ks), (L, D), 0.01),
        'ln1_b': _normal(next(ks), (L, D), 0.01),
        'peer_w_q': _normal(next(ks), (L, D, PEER_HEADS * PEER_DKEY), D ** -0.5),
        'peer_sub_keys': _normal(next(ks), (L, PEER_HEADS, 2, PEER_KEYS, PEER_DKEY // 2), (PEER_DKEY // 2) ** -0.5),
        'peer_u': _normal(next(ks), (L, PEER_EXPERTS, D), D ** -0.5),
        'peer_v': _normal(next(ks), (L, PEER_EXPERTS, D), beta),
        'ln2_g': 1.0 + _normal(next(ks), (L, D), 0.01),
        'ln2_b': _normal(next(ks), (L, D), 0.01),
    }


def reference(x, c, ctx, c_ctx, w_ada, b_ada, w_in, mla_q_norm, mla_w_uq, mla_kv_norm, mla_w_ukv, gqa_q_norm, gqa_k_norm, ssd_conv_w, ssd_conv_b, ssd_a_log, ssd_dt_bias, ssd_d, ssd_norm, swa_sink, w_out, ln1_g, ln1_b, peer_w_q, peer_sub_keys, peer_u, peer_v, ln2_g, ln2_b):
    rows = x.shape[1] // GRID_W
    rope_mla = axial_rope_tables(rows, MLA_ROPE)
    rope_head = axial_rope_tables(rows, HEAD_DIM)
    alpha = (2.0 * DEPTH) ** 0.25
    silu_c = jax.nn.silu(c)[:, None, :]
    silu_cc = jax.nn.silu(c_ctx)[None, None, :]
    for l in range(DEPTH):
        ctx_out = l < DEPTH - 1
        mx = jnp.split(silu_c @ w_ada[l] + b_ada[l], 6, axis=-1)
        mc = jnp.split(silu_cc @ w_ada[l] + b_ada[l], 6, axis=-1)
        px = jnp.split(modulate(x, mx[0], mx[1]) @ w_in[l], GROUP_SPLITS, axis=-1)
        pc = jnp.split(modulate(ctx, mc[0], mc[1]) @ w_in[l], GROUP_SPLITS, axis=-1)
        ya_x, ya_c = mla_mixer(px[0], pc[0], mla_q_norm[l], mla_w_uq[l], mla_kv_norm[l], mla_w_ukv[l], rope_mla, ctx_out)
        yb_x, yb_c = gqa_mixer(px[1], pc[1], gqa_q_norm[l], gqa_k_norm[l], rope_head, ctx_out)
        yc_x, yc_c = ssd_mixer(px[2], pc[2], ssd_conv_w[l], ssd_conv_b[l], ssd_a_log[l], ssd_dt_bias[l], ssd_d[l], ssd_norm[l], ctx_out)
        yd_x, yd_c = swa_mixer(px[3], pc[3], swa_sink[l], rope_head, ctx_out)
        mix_x = jnp.concatenate([ya_x, yb_x, yc_x, yd_x], axis=-1) @ w_out[l]
        x = layer_norm(alpha * x + mx[2] * mix_x, ln1_g[l], ln1_b[l])
        ffn_x = peer_ffn(modulate(x, mx[3], mx[4]), peer_w_q[l], peer_sub_keys[l], peer_u[l], peer_v[l])
        x = layer_norm(alpha * x + mx[5] * ffn_x, ln2_g[l], ln2_b[l])
        if ctx_out:
            mix_c = jnp.concatenate([ya_c, yb_c, yc_c, yd_c], axis=-1) @ w_out[l]
            ctx = layer_norm(alpha * ctx + mc[2] * mix_c, ln1_g[l], ln1_b[l])
            ffn_c = peer_ffn(modulate(ctx, mc[3], mc[4]), peer_w_q[l], peer_sub_keys[l], peer_u[l], peer_v[l])
            ctx = layer_norm(alpha * ctx + mc[5] * ffn_c, ln2_g[l], ln2_b[l])
    return x
```

```python
import functools
import math

import jax
import jax.numpy as jnp
from jax import lax
from jax.experimental import pallas as pl
from jax.experimental.pallas import tpu as pltpu

GRID_W = 64
ROPE_BASE = 10000.0
EPS = 1e-6
HEAD_DIM = 128

MLA_HEADS = 4
MLA_Q_RANK = 512
MLA_KV_RANK = 256
MLA_NOPE = 128
MLA_ROPE = 64
MLA_V = 128
MLA_QK_PAD = 256

GQA_HEADS = 4
GQA_KV_HEADS = 2

SSD_HEADS = 8
SSD_HEAD_DIM = 64
SSD_INNER = SSD_HEADS * SSD_HEAD_DIM
SSD_GROUPS = 2
SSD_STATE = 128
SSD_CONV = 5
SSD_CHUNK = 128
SSD_CONV_DIM = SSD_INNER + 2 * SSD_GROUPS * SSD_STATE

SWA_HEADS = 4
SWA_KV_HEADS = 2
WINDOW = 128

PEER_HEADS = 8
PEER_KEYS = 128
PEER_TOPK = 16
PEER_DKEY = 256
PEER_BLOCK = 128

MLA_COLS = MLA_Q_RANK + MLA_KV_RANK + MLA_ROPE
GQA_COLS = (GQA_HEADS + 2 * GQA_KV_HEADS) * HEAD_DIM
SSD_COLS = SSD_INNER + SSD_CONV_DIM + 2 * SSD_HEADS
SWA_COLS = (SWA_HEADS + 2 * SWA_KV_HEADS) * HEAD_DIM
GROUP_SPLITS = [MLA_COLS, MLA_COLS + GQA_COLS, MLA_COLS + GQA_COLS + SSD_COLS]

VMEM_LIMIT_BYTES = 56 * 1024 * 1024
LANES = 128

F32 = jnp.float32
BF16 = jnp.bfloat16


def _largest_tile(n, cap, unit):
    return max(t for t in range(unit, min(n, cap) + 1, unit) if n % t == 0)


def _linear_mod_kernel(x_ref, sh_ref, sc_ref, w_ref, o_ref):
    xm = x_ref[0] * (1.0 + sc_ref[0]) + sh_ref[0]
    o_ref[0] = jnp.dot(xm.astype(BF16), w_ref[...], preferred_element_type=F32)


def _linear_kernel(x_ref, w_ref, o_ref):
    o_ref[0] = jnp.dot(x_ref[0].astype(BF16), w_ref[...], preferred_element_type=F32)


def _linear(x, w, shift=None, scale=None, tm=512):
    bsz, t_len, k_dim = x.shape
    n_dim = w.shape[1]
    n_pad = -(-n_dim // LANES) * LANES
    if n_pad != n_dim:
        w = jnp.pad(w, ((0, 0), (0, n_pad - n_dim)))
    tn = _largest_tile(n_pad, 1024, LANES)
    tm = _largest_tile(t_len, tm, 8)
    grid = (bsz, t_len // tm, n_pad // tn)
    x_spec = pl.BlockSpec((1, tm, k_dim), lambda b, i, j: (b, i, 0))
    w_spec = pl.BlockSpec((k_dim, tn), lambda b, i, j: (0, j))
    o_spec = pl.BlockSpec((1, tm, tn), lambda b, i, j: (b, i, j))
    params = pltpu.CompilerParams(dimension_semantics=("arbitrary", "arbitrary", "arbitrary"),
                                  vmem_limit_bytes=VMEM_LIMIT_BYTES)
    out_shape = jax.ShapeDtypeStruct((bsz, t_len, n_pad), F32)
    if shift is None:
        out = pl.pallas_call(_linear_kernel, grid=grid, in_specs=[x_spec, w_spec], out_specs=o_spec,
                             out_shape=out_shape, compiler_params=params)(x, w)
    else:
        m_spec = pl.BlockSpec((1, 1, k_dim), lambda b, i, j: (b, 0, 0))
        out = pl.pallas_call(_linear_mod_kernel, grid=grid, in_specs=[x_spec, m_spec, m_spec, w_spec], out_specs=o_spec,
                             out_shape=out_shape, compiler_params=params)(x, shift, scale, w)
    return out if n_pad == n_dim else out[..., :n_dim]


def _dense_attn_kernel(q_ref, k_ref, v_ref, o_ref, m_ref, l_ref, acc_ref, *, groups, dk, dv, tk, scale):
    tq = q_ref.shape[1]
    n_chunks = k_ref.shape[1] // tk
    q = q_ref[0] * scale
    q = jnp.concatenate([q[:, g * dk:(g + 1) * dk] for g in range(groups)], axis=0).astype(BF16)
    m_ref[...] = jnp.full(m_ref.shape, -jnp.inf, F32)
    l_ref[...] = jnp.zeros(l_ref.shape, F32)
    acc_ref[...] = jnp.zeros(acc_ref.shape, F32)

    def chunk(j, carry):
        start = pl.multiple_of(j * tk, tk)
        kb = k_ref[0, pl.ds(start, tk), :]
        vb = v_ref[0, pl.ds(start, tk), :]
        s = lax.dot_general(q, kb, (((1,), (1,)), ((), ())), preferred_element_type=F32)
        m_old = m_ref[...]
        m_new = jnp.maximum(m_old, jnp.max(s, axis=-1, keepdims=True))
        p = jnp.exp(s - m_new)
        alpha = jnp.exp(m_old - m_new)
        l_ref[...] = alpha * l_ref[...] + jnp.sum(p, axis=-1, keepdims=True)
        acc_ref[...] = alpha * acc_ref[...] + jnp.dot(p.astype(BF16), vb, preferred_element_type=F32)
        m_ref[...] = m_new
        return carry

    lax.fori_loop(0, n_chunks, chunk, 0)
    o = acc_ref[...] / l_ref[...]
    o_ref[0] = jnp.concatenate([o[g * tq:(g + 1) * tq] for g in range(groups)], axis=1)


def _dense_attention(q, k, v, *, kv_heads, groups, dk, dv, scale, rows=512, tk=640):
    bsz, s_len, _ = q.shape
    sk = k.shape[1]
    tq = _largest_tile(s_len, rows // groups, 8)
    tk = _largest_tile(sk, tk, LANES)
    kern = functools.partial(_dense_attn_kernel, groups=groups, dk=dk, dv=dv, tk=tk, scale=scale)
    return pl.pallas_call(
        kern,
        grid=(bsz, kv_heads, s_len // tq),
        in_specs=[
            pl.BlockSpec((1, tq, groups * dk), lambda b, h, i: (b, i, h)),
            pl.BlockSpec((1, sk, dk), lambda b, h, i: (b, 0, h)),
            pl.BlockSpec((1, sk, dv), lambda b, h, i: (b, 0, h)),
        ],
        out_specs=pl.BlockSpec((1, tq, groups * dv), lambda b, h, i: (b, i, h)),
        out_shape=jax.ShapeDtypeStruct((bsz, s_len, kv_heads * groups * dv), F32),
        scratch_shapes=[
            pltpu.VMEM((groups * tq, 1), F32),
            pltpu.VMEM((groups * tq, 1), F32),
            pltpu.VMEM((groups * tq, dv), F32),
        ],
        compiler_params=pltpu.CompilerParams(dimension_semantics=("arbitrary", "arbitrary", "arbitrary"),
                                             vmem_limit_bytes=VMEM_LIMIT_BYTES),
    )(q, k, v)


def _rms_norm(x, g):
    xf = x.astype(F32)
    y = xf * lax.rsqrt(jnp.mean(xf * xf, axis=-1, keepdims=True) + EPS)
    return (y * g.astype(F32)).astype(x.dtype)


def _layer_norm(x, g, b):
    xf = x.astype(F32)
    mu = jnp.mean(xf, axis=-1, keepdims=True)
    var = jnp.mean(jnp.square(xf - mu), axis=-1, keepdims=True)
    y = (xf - mu) * lax.rsqrt(var + EPS) * g.astype(F32) + b.astype(F32)
    return y.astype(x.dtype)


def _modulate(x, shift, scale):
    return x * (1 + scale) + shift


def _axial_rope_tables(rows, rot_dim):
    row = jnp.repeat(jnp.arange(rows, dtype=F32), GRID_W)
    col = jnp.tile(jnp.arange(GRID_W, dtype=F32), rows)
    n_freq = rot_dim // 4
    inv_freq = ROPE_BASE ** (-jnp.arange(n_freq, dtype=F32) / n_freq)
    ang = jnp.concatenate([row[:, None] * inv_freq, col[:, None] * inv_freq], axis=-1)
    return jnp.cos(ang), jnp.sin(ang)


def _apply_rope(x, cos, sin):
    half = x.shape[-1] // 2
    x1, x2 = x[..., :half], x[..., half:]
    c = cos[None, :, None, :].astype(x.dtype)
    s = sin[None, :, None, :].astype(x.dtype)
    return jnp.concatenate([x1 * c - x2 * s, x1 * s + x2 * c], axis=-1)


def _attn_heads(p, n_q, n_kv):
    bsz, t_len, _ = p.shape
    q, k, v = jnp.split(p, [n_q * HEAD_DIM, (n_q + n_kv) * HEAD_DIM], axis=-1)
    return (q.reshape(bsz, t_len, n_q, HEAD_DIM), k.reshape(bsz, t_len, n_kv, HEAD_DIM), v.reshape(bsz, t_len, n_kv, HEAD_DIM))


def _group_q(q, n_kv):
    bsz, t_len, h, d = q.shape
    return q.reshape(bsz, t_len, n_kv, h // n_kv, d)


def _flat(a):
    return a.reshape(a.shape[0], a.shape[1], -1)


def _mla_queries(p, q_norm, w_uq, rope):
    bsz, t_len, _ = p.shape
    q = _linear(_rms_norm(p[..., :MLA_Q_RANK], q_norm), w_uq).reshape(bsz, t_len, MLA_HEADS, MLA_NOPE + MLA_ROPE)
    if rope is not None:
        q = jnp.concatenate([q[..., :MLA_NOPE], _apply_rope(q[..., MLA_NOPE:], *rope)], axis=-1)
    pad = jnp.zeros((bsz, t_len, MLA_HEADS, MLA_QK_PAD - MLA_NOPE - MLA_ROPE), q.dtype)
    return _flat(jnp.concatenate([q, pad], axis=-1))


def _mla_keys_values(p, kv_norm, w_ukv, rope):
    bsz, t_len, _ = p.shape
    ckv = p[..., MLA_Q_RANK:MLA_Q_RANK + MLA_KV_RANK]
    k_rope = p[..., MLA_Q_RANK + MLA_KV_RANK:][:, :, None, :]
    if rope is not None:
        k_rope = _apply_rope(k_rope, *rope)
    kv = _linear(_rms_norm(ckv, kv_norm), w_ukv).reshape(bsz, t_len, MLA_HEADS, MLA_NOPE + MLA_V)
    pad = jnp.zeros((bsz, t_len, MLA_HEADS, MLA_QK_PAD - MLA_NOPE - MLA_ROPE), kv.dtype)
    k = jnp.concatenate([kv[..., :MLA_NOPE], jnp.broadcast_to(k_rope, (bsz, t_len, MLA_HEADS, MLA_ROPE)), pad], axis=-1)
    return _flat(k).astype(BF16), _flat(kv[..., MLA_NOPE:]).astype(BF16)


def _mla_mixer(px, pc, q_norm, w_uq, kv_norm, w_ukv, rope, ctx_out):
    scale = (MLA_NOPE + MLA_ROPE) ** -0.5
    attn = functools.partial(_dense_attention, kv_heads=MLA_HEADS, groups=1, dk=MLA_QK_PAD, dv=MLA_V, scale=scale)
    kx, vx = _mla_keys_values(px, kv_norm, w_ukv, rope)
    kc, vc = _mla_keys_values(pc, kv_norm, w_ukv, None)
    qx = _mla_queries(px, q_norm, w_uq, rope)
    y_x = attn(qx, jnp.concatenate([kc, kx], axis=1), jnp.concatenate([vc, vx], axis=1))
    y_c = attn(_mla_queries(pc, q_norm, w_uq, None), kc, vc) if ctx_out else None
    return y_x, y_c


def _gqa_mixer(px, pc, q_norm, k_norm, rope, ctx_out):
    scale = HEAD_DIM ** -0.5
    attn = functools.partial(_dense_attention, kv_heads=GQA_KV_HEADS, groups=GQA_HEADS // GQA_KV_HEADS, dk=HEAD_DIM,
                             dv=HEAD_DIM, scale=scale)
    qx, kx, vx = _attn_heads(px, GQA_HEADS, GQA_KV_HEADS)
    qc, kc, vc = _attn_heads(pc, GQA_HEADS, GQA_KV_HEADS)
    qx = _apply_rope(_rms_norm(qx, q_norm), *rope)
    kx = _apply_rope(_rms_norm(kx, k_norm), *rope)
    kc = _rms_norm(kc, k_norm)
    k_all = _flat(jnp.concatenate([kc, kx], axis=1)).astype(BF16)
    v_all = _flat(jnp.concatenate([vc, vx], axis=1)).astype(BF16)
    y_x = attn(_flat(qx), k_all, v_all)
    y_c = attn(_flat(_rms_norm(qc, q_norm)), _flat(kc).astype(BF16), _flat(vc).astype(BF16)) if ctx_out else None
    return y_x, y_c


def _window_attention(q, k, v, kc, vc, sink, scale):
    bsz, s_len, hk, g, d = q.shape
    nb = s_len // WINDOW
    qb = q.reshape(bsz, nb, WINDOW, hk, g, d)
    pad = ((0, 0), (WINDOW, WINDOW), (0, 0), (0, 0))
    kp = jnp.pad(k, pad).reshape(bsz, nb + 2, WINDOW, hk, d)
    vp = jnp.pad(v, pad).reshape(bsz, nb + 2, WINDOW, hk, d)
    kband = jnp.concatenate([kp[:, :-2], kp[:, 1:-1], kp[:, 2:]], axis=2)
    vband = jnp.concatenate([vp[:, :-2], vp[:, 1:-1], vp[:, 2:]], axis=2)
    s_win = jnp.einsum('bnqhgd,bnkhd->bnhgqk', qb, kband).astype(F32) * scale
    blk = jnp.arange(nb)[:, None] * WINDOW
    q_pos = blk + jnp.arange(WINDOW)[None, :]
    k_pos = blk - WINDOW + jnp.arange(3 * WINDOW)[None, :]
    valid = (jnp.abs(k_pos[:, None, :] - q_pos[:, :, None]) <= WINDOW) & (k_pos >= 0)[:, None, :] & (k_pos < s_len)[:, None, :]
    s_win = jnp.where(valid[None, :, None, None], s_win, -jnp.inf)
    s_ctx = jnp.einsum('bnqhgd,bchd->bnhgqc', qb, kc).astype(F32) * scale
    s_sink = jnp.broadcast_to(sink.reshape(hk, g)[None, None, :, :, None, None].astype(F32), s_ctx.shape[:-1] + (1,))
    p = jax.nn.softmax(jnp.concatenate([s_ctx, s_win, s_sink], axis=-1), axis=-1).astype(v.dtype)
    n_ctx = kc.shape[1]
    o = jnp.einsum('bnhgqc,bchd->bnqhgd', p[..., :n_ctx], vc) + jnp.einsum('bnhgqk,bnkhd->bnqhgd', p[..., n_ctx:n_ctx + 3 * WINDOW], vband)
    return o.reshape(bsz, s_len, hk * g * d)


def _sink_attention(q, k, v, sink, scale):
    bsz, t_len, hk, g, d = q.shape
    s = jnp.einsum('bqhgd,bkhd->bhgqk', q, k).astype(F32) * scale
    s_sink = jnp.broadcast_to(sink.reshape(hk, g)[None, :, :, None, None].astype(F32), s.shape[:-1] + (1,))
    p = jax.nn.softmax(jnp.concatenate([s, s_sink], axis=-1), axis=-1)[..., :-1].astype(v.dtype)
    return jnp.einsum('bhgqk,bkhd->bqhgd', p, v).reshape(bsz, t_len, hk * g * d)


def _swa_mixer(px, pc, sink, rope, ctx_out):
    scale = HEAD_DIM ** -0.5
    qx, kx, vx = _attn_heads(px, SWA_HEADS, SWA_KV_HEADS)
    qc, kc, vc = _attn_heads(pc, SWA_HEADS, SWA_KV_HEADS)
    qx = _apply_rope(qx, *rope)
    kx = _apply_rope(kx, *rope)
    y_x = _window_attention(_group_q(qx, SWA_KV_HEADS), kx, vx, kc, vc, sink, scale)
    y_c = _sink_attention(_group_q(qc, SWA_KV_HEADS), kc, vc, sink, scale) if ctx_out else None
    return y_x, y_c


def _centred_dwconv(x, w, b):
    y = lax.conv_general_dilated(x, w[:, None, :], window_strides=(1,), padding=((SSD_CONV // 2, SSD_CONV // 2),),
                                 dimension_numbers=('NWC', 'WIO', 'NWC'), feature_group_count=x.shape[-1])
    return y + b


def _ssd_chunked(x, dt, A, Bm, Cm, h0, with_output):
    bsz, t_len = x.shape[:2]
    L, G, Hg, P, N = SSD_CHUNK, SSD_GROUPS, SSD_HEADS // SSD_GROUPS, SSD_HEAD_DIM, SSD_STATE
    nc = t_len // L
    xc = x.astype(F32).reshape(bsz, nc, L, G, Hg, P)
    dtc = dt.astype(F32).reshape(bsz, nc, L, G, Hg)
    Bc = Bm.astype(F32).reshape(bsz, nc, L, G, N)
    Cc = Cm.astype(F32).reshape(bsz, nc, L, G, N)
    a_cum = jnp.cumsum(dtc * A.astype(F32).reshape(G, Hg), axis=2)
    a_end = a_cum[:, :, -1]
    states = jnp.einsum('bclgn,bclgh,bclghp->bcghpn', Bc, jnp.exp(a_end[:, :, None] - a_cum) * dtc, xc)

    def carry_state(h, inp):
        decay, st = inp
        return h * decay[..., None, None] + st, h

    h_final, h_enter = lax.scan(carry_state, h0.astype(F32), (jnp.moveaxis(jnp.exp(a_end), 1, 0), jnp.moveaxis(states, 1, 0)))
    if not with_output:
        return None, h_final
    h_enter = jnp.moveaxis(h_enter, 0, 1)
    causal = jnp.tril(jnp.ones((L, L), dtype=bool))
    seg = a_cum[:, :, :, None] - a_cum[:, :, None]
    decay = jnp.exp(jnp.where(causal[:, :, None, None], seg, -jnp.inf))
    cb = jnp.einsum('bclgn,bcsgn->bclsg', Cc, Bc)
    y_diag = jnp.einsum('bclsgh,bcsghp->bclghp', cb[..., None] * decay * dtc[:, :, None], xc)
    y_off = jnp.einsum('bclgn,bcghpn->bclghp', Cc, h_enter) * jnp.exp(a_cum)[..., None]
    return (y_diag + y_off).reshape(bsz, t_len, SSD_HEADS, P), h_final


def _ssd_mixer(px, pc, conv_w, conv_b, a_log, dt_bias, d_skip, norm_g, ctx_out):
    def prep(p):
        bsz, t_len, _ = p.shape
        z, xbc, dt = jnp.split(p, [SSD_INNER, SSD_INNER + SSD_CONV_DIM], axis=-1)
        xbc = jax.nn.silu(_centred_dwconv(xbc, conv_w, conv_b))
        xs, Bm, Cm = jnp.split(xbc, [SSD_INNER, SSD_INNER + SSD_GROUPS * SSD_STATE], axis=-1)
        xs = xs.reshape(bsz, t_len, SSD_HEADS, SSD_HEAD_DIM)
        Bm = Bm.reshape(bsz, t_len, SSD_GROUPS, SSD_STATE)
        Cm = Cm.reshape(bsz, t_len, SSD_GROUPS, SSD_STATE)
        dt = jax.nn.softplus(dt.reshape(bsz, t_len, 2, SSD_HEADS) + dt_bias)
        return z, xs, Bm, Cm, dt

    def flip(a):
        return jnp.flip(a, axis=1)

    A = -jnp.exp(a_log.astype(F32))
    zx, xx, Bx, Cx, dtx = prep(px)
    zc, xc, Bc, Cc, dtc = prep(pc)
    h0 = jnp.zeros((px.shape[0], SSD_GROUPS, SSD_HEADS // SSD_GROUPS, SSD_HEAD_DIM, SSD_STATE), F32)
    yc_f, hc_f = _ssd_chunked(xc, dtc[:, :, 0], A[0], Bc, Cc, h0, ctx_out)
    yx_f, _ = _ssd_chunked(xx, dtx[:, :, 0], A[0], Bx, Cx, hc_f, True)
    yc_b, hc_b = _ssd_chunked(flip(xc), flip(dtc[:, :, 1]), A[1], flip(Bc), flip(Cc), h0, ctx_out)
    yx_b, _ = _ssd_chunked(flip(xx), flip(dtx[:, :, 1]), A[1], flip(Bx), flip(Cx), hc_b, True)

    def finish(y_f, y_b, xs, z):
        y = (y_f + flip(y_b) + d_skip.astype(F32)[:, None] * xs.astype(F32)).astype(xs.dtype)
        bsz, t_len = xs.shape[:2]
        return _rms_norm(y.reshape(bsz, t_len, SSD_INNER) * jax.nn.silu(z), norm_g)

    y_x = finish(yx_f, yx_b, xx, zx)
    y_c = finish(yc_f, yc_b, xc, zc) if ctx_out else None
    return y_x, y_c


def _peer_ffn(h, shift, scale, w_q, sub_keys, u, v):
    shape = h.shape
    qf = _linear(h, w_q, shift, scale)
    hb = _modulate(h, shift, scale).reshape(-1, PEER_BLOCK, shape[-1])
    qb = qf.reshape(-1, PEER_BLOCK, qf.shape[-1])

    def one_block(args):
        t, q = args
        q = q.reshape(PEER_BLOCK, PEER_HEADS, 2, PEER_DKEY // 2)
        s = jnp.einsum('thpd,hpkd->thpk', q, sub_keys).astype(F32)
        s1, i1 = lax.top_k(s[:, :, 0], PEER_TOPK)
        s2, i2 = lax.top_k(s[:, :, 1], PEER_TOPK)
        cand = (s1[..., :, None] + s2[..., None, :]).reshape(PEER_BLOCK, PEER_HEADS, PEER_TOPK * PEER_TOPK)
        sc, ci = lax.top_k(cand, PEER_TOPK)
        e = jnp.take_along_axis(i1, ci // PEER_TOPK, axis=-1) * PEER_KEYS + jnp.take_along_axis(i2, ci % PEER_TOPK, axis=-1)
        g = jax.nn.softmax(sc, axis=-1).astype(t.dtype)
        act = jax.nn.gelu(jnp.einsum('thkd,td->thk', u[e], t))
        return jnp.einsum('thk,thkd->td', g * act, v[e])

    return lax.map(one_block, (hb, qb)).reshape(shape)


def kernel(x, c, ctx, c_ctx, w_ada, b_ada, w_in, mla_q_norm, mla_w_uq, mla_kv_norm, mla_w_ukv, gqa_q_norm, gqa_k_norm, ssd_conv_w, ssd_conv_b, ssd_a_log, ssd_dt_bias, ssd_d, ssd_norm, swa_sink, w_out, ln1_g, ln1_b, peer_w_q, peer_sub_keys, peer_u, peer_v, ln2_g, ln2_b):
    bsz = x.shape[0]
    depth = w_in.shape[0]
    rows = x.shape[1] // GRID_W
    rope_mla = _axial_rope_tables(rows, MLA_ROPE)
    rope_head = _axial_rope_tables(rows, HEAD_DIM)
    alpha = (2.0 * depth) ** 0.25
    silu_c = jax.nn.silu(c)[:, None, :]
    silu_cc = jax.nn.silu(c_ctx)[None, None, :]
    for l in range(depth):
        ctx_out = l < depth - 1
        mx = jnp.split(silu_c @ w_ada[l] + b_ada[l], 6, axis=-1)
        mc = [jnp.broadcast_to(m, (bsz,) + m.shape[1:]) for m in jnp.split(silu_cc @ w_ada[l] + b_ada[l], 6, axis=-1)]
        w_in_l = w_in[l].astype(BF16)
        px = jnp.split(_linear(x, w_in_l, mx[0], mx[1]), GROUP_SPLITS, axis=-1)
        pc = jnp.split(_linear(ctx, w_in_l, mc[0], mc[1]), GROUP_SPLITS, axis=-1)
        w_uq = mla_w_uq[l].astype(BF16)
        w_ukv = mla_w_ukv[l].astype(BF16)
        ya_x, ya_c = _mla_mixer(px[0], pc[0], mla_q_norm[l], w_uq, mla_kv_norm[l], w_ukv, rope_mla, ctx_out)
        yb_x, yb_c = _gqa_mixer(px[1], pc[1], gqa_q_norm[l], gqa_k_norm[l], rope_head, ctx_out)
        yc_x, yc_c = _ssd_mixer(px[2], pc[2], ssd_conv_w[l], ssd_conv_b[l], ssd_a_log[l], ssd_dt_bias[l], ssd_d[l], ssd_norm[l], ctx_out)
        yd_x, yd_c = _swa_mixer(px[3], pc[3], swa_sink[l], rope_head, ctx_out)
        w_out_l = w_out[l].astype(BF16)
        w_pq = peer_w_q[l].astype(BF16)
        mix_x = _linear(jnp.concatenate([ya_x, yb_x, yc_x, yd_x], axis=-1), w_out_l)
        x = _layer_norm(alpha * x + mx[2] * mix_x, ln1_g[l], ln1_b[l])
        ffn_x = _peer_ffn(x, mx[3], mx[4], w_pq, peer_sub_keys[l], peer_u[l], peer_v[l])
        x = _layer_norm(alpha * x + mx[5] * ffn_x, ln2_g[l], ln2_b[l])
        if ctx_out:
            mix_c = _linear(jnp.concatenate([ya_c, yb_c, yc_c, yd_c], axis=-1), w_out_l)
            ctx = _layer_norm(alpha * ctx + mc[2] * mix_c, ln1_g[l], ln1_b[l])
            ffn_c = _peer_ffn(ctx, mc[3], mc[4], w_pq, peer_sub_keys[l], peer_u[l], peer_v[l])
            ctx = _layer_norm(alpha * ctx + mc[5] * ffn_c, ln2_g[l], ln2_b[l])
    return x
```

```python
import functools
import math

import jax
import jax.numpy as jnp
from jax import lax
from jax.experimental import pallas as pl
from jax.experimental.pallas import tpu as pltpu

GRID_W = 64
ROPE_BASE = 10000.0
EPS = 1e-6
HEAD_DIM = 128

MLA_HEADS = 4
MLA_Q_RANK = 512
MLA_KV_RANK = 256
MLA_NOPE = 128
MLA_ROPE = 64
MLA_V = 128
MLA_QK_PAD = 256

GQA_HEADS = 4
GQA_KV_HEADS = 2

SSD_HEADS = 8
SSD_HEAD_DIM = 64
SSD_INNER = SSD_HEADS * SSD_HEAD_DIM
SSD_GROUPS = 2
SSD_STATE = 128
SSD_CONV = 5
SSD_CHUNK = 128
SSD_CONV_DIM = SSD_INNER + 2 * SSD_GROUPS * SSD_STATE

SWA_HEADS = 4
SWA_KV_HEADS = 2
WINDOW = 128

PEER_HEADS = 8
PEER_KEYS = 128
PEER_TOPK = 16
PEER_DKEY = 256
PEER_BLOCK = 128

MLA_COLS = MLA_Q_RANK + MLA_KV_RANK + MLA_ROPE
GQA_COLS = (GQA_HEADS + 2 * GQA_KV_HEADS) * HEAD_DIM
SSD_COLS = SSD_INNER + SSD_CONV_DIM + 2 * SSD_HEADS
SWA_COLS = (SWA_HEADS + 2 * SWA_KV_HEADS) * HEAD_DIM
GROUP_SPLITS = [MLA_COLS, MLA_COLS + GQA_COLS, MLA_COLS + GQA_COLS + SSD_COLS]

VMEM_LIMIT_BYTES = 56 * 1024 * 1024
LANES = 128

F32 = jnp.float32
BF16 = jnp.bfloat16


def _largest_tile(n, cap, unit):
    return max(t for t in range(unit, min(n, cap) + 1, unit) if n % t == 0)


def _linear_mod_kernel(x_ref, sh_ref, sc_ref, w_ref, o_ref):
    xm = x_ref[0] * (1.0 + sc_ref[0]) + sh_ref[0]
    o_ref[0] = jnp.dot(xm.astype(BF16), w_ref[...], preferred_element_type=F32)


def _linear_kernel(x_ref, w_ref, o_ref):
    o_ref[0] = jnp.dot(x_ref[0].astype(BF16), w_ref[...], preferred_element_type=F32)


def _linear(x, w, shift=None, scale=None, tm=512):
    bsz, t_len, k_dim = x.shape
    n_dim = w.shape[1]
    n_pad = -(-n_dim // LANES) * LANES
    if n_pad != n_dim:
        w = jnp.pad(w, ((0, 0), (0, n_pad - n_dim)))
    tn = _largest_tile(n_pad, 1024, LANES)
    tm = _largest_tile(t_len, tm, 8)
    grid = (bsz, t_len // tm, n_pad // tn)
    x_spec = pl.BlockSpec((1, tm, k_dim), lambda b, i, j: (b, i, 0))
    w_spec = pl.BlockSpec((k_dim, tn), lambda b, i, j: (0, j))
    o_spec = pl.BlockSpec((1, tm, tn), lambda b, i, j: (b, i, j))
    params = pltpu.CompilerParams(dimension_semantics=("arbitrary", "arbitrary", "arbitrary"),
                                  vmem_limit_bytes=VMEM_LIMIT_BYTES)
    out_shape = jax.ShapeDtypeStruct((bsz, t_len, n_pad), F32)
    if shift is None:
        out = pl.pallas_call(_linear_kernel, grid=grid, in_specs=[x_spec, w_spec], out_specs=o_spec,
                             out_shape=out_shape, compiler_params=params, name="linear")(x, w)
    else:
        m_spec = pl.BlockSpec((1, 1, k_dim), lambda b, i, j: (b, 0, 0))
        out = pl.pallas_call(_linear_mod_kernel, grid=grid, in_specs=[x_spec, m_spec, m_spec, w_spec], out_specs=o_spec,
                             out_shape=out_shape, compiler_params=params, name="linear_mod")(x, shift, scale, w)
    return out if n_pad == n_dim else out[..., :n_dim]


def _dense_attn_kernel(q_ref, k_ref, v_ref, o_ref, m_ref, l_ref, acc_ref, *, groups, dk, dv, tk, scale):
    tq = q_ref.shape[1]
    n_chunks = k_ref.shape[1] // tk
    q = q_ref[0] * scale
    q = jnp.concatenate([q[:, g * dk:(g + 1) * dk] for g in range(groups)], axis=0).astype(BF16)
    m_ref[...] = jnp.full(m_ref.shape, -jnp.inf, F32)
    l_ref[...] = jnp.zeros(l_ref.shape, F32)
    acc_ref[...] = jnp.zeros(acc_ref.shape, F32)

    def chunk(j, carry):
        start = pl.multiple_of(j * tk, tk)
        kb = k_ref[0, pl.ds(start, tk), :]
        vb = v_ref[0, pl.ds(start, tk), :]
        s = lax.dot_general(q, kb, (((1,), (1,)), ((), ())), preferred_element_type=F32)
        m_old = m_ref[...]
        m_new = jnp.maximum(m_old, jnp.max(s, axis=-1, keepdims=True))
        p = jnp.exp(s - m_new)
        alpha = jnp.exp(m_old - m_new)
        l_ref[...] = alpha * l_ref[...] + jnp.sum(p, axis=-1, keepdims=True)
        acc_ref[...] = alpha * acc_ref[...] + jnp.dot(p.astype(BF16), vb, preferred_element_type=F32)
        m_ref[...] = m_new
        return carry

    lax.fori_loop(0, n_chunks, chunk, 0)
    o = acc_ref[...] / l_ref[...]
    o_ref[0] = jnp.concatenate([o[g * tq:(g + 1) * tq] for g in range(groups)], axis=1)


def _dense_attention(q, k, v, *, kv_heads, groups, dk, dv, scale, rows=512, tk=640):
    bsz, s_len, _ = q.shape
    sk = k.shape[1]
    tq = _largest_tile(s_len, rows // groups, 8)
    tk = _largest_tile(sk, tk, LANES)
    kern = functools.partial(_dense_attn_kernel, groups=groups, dk=dk, dv=dv, tk=tk, scale=scale)
    return pl.pallas_call(
        kern,
        name="dense_attn",
        grid=(bsz, kv_heads, s_len // tq),
        in_specs=[
            pl.BlockSpec((1, tq, groups * dk), lambda b, h, i: (b, i, h)),
            pl.BlockSpec((1, sk, dk), lambda b, h, i: (b, 0, h)),
            pl.BlockSpec((1, sk, dv), lambda b, h, i: (b, 0, h)),
        ],
        out_specs=pl.BlockSpec((1, tq, groups * dv), lambda b, h, i: (b, i, h)),
        out_shape=jax.ShapeDtypeStruct((bsz, s_len, kv_heads * groups * dv), F32),
        scratch_shapes=[
            pltpu.VMEM((groups * tq, 1), F32),
            pltpu.VMEM((groups * tq, 1), F32),
            pltpu.VMEM((groups * tq, dv), F32),
        ],
        compiler_params=pltpu.CompilerParams(dimension_semantics=("arbitrary", "arbitrary", "arbitrary"),
                                             vmem_limit_bytes=VMEM_LIMIT_BYTES),
    )(q, k, v)


def _rms_norm(x, g):
    xf = x.astype(F32)
    y = xf * lax.rsqrt(jnp.mean(xf * xf, axis=-1, keepdims=True) + EPS)
    return (y * g.astype(F32)).astype(x.dtype)


def _layer_norm(x, g, b):
    xf = x.astype(F32)
    mu = jnp.mean(xf, axis=-1, keepdims=True)
    var = jnp.mean(jnp.square(xf - mu), axis=-1, keepdims=True)
    y = (xf - mu) * lax.rsqrt(var + EPS) * g.astype(F32) + b.astype(F32)
    return y.astype(x.dtype)


def _modulate(x, shift, scale):
    return x * (1 + scale) + shift


def _axial_rope_tables(rows, rot_dim):
    row = jnp.repeat(jnp.arange(rows, dtype=F32), GRID_W)
    col = jnp.tile(jnp.arange(GRID_W, dtype=F32), rows)
    n_freq = rot_dim // 4
    inv_freq = ROPE_BASE ** (-jnp.arange(n_freq, dtype=F32) / n_freq)
    ang = jnp.concatenate([row[:, None] * inv_freq, col[:, None] * inv_freq], axis=-1)
    return jnp.cos(ang), jnp.sin(ang)


def _apply_rope(x, cos, sin):
    half = x.shape[-1] // 2
    x1, x2 = x[..., :half], x[..., half:]
    c = cos[None, :, None, :].astype(x.dtype)
    s = sin[None, :, None, :].astype(x.dtype)
    return jnp.concatenate([x1 * c - x2 * s, x1 * s + x2 * c], axis=-1)


def _attn_heads(p, n_q, n_kv):
    bsz, t_len, _ = p.shape
    q, k, v = jnp.split(p, [n_q * HEAD_DIM, (n_q + n_kv) * HEAD_DIM], axis=-1)
    return (q.reshape(bsz, t_len, n_q, HEAD_DIM), k.reshape(bsz, t_len, n_kv, HEAD_DIM), v.reshape(bsz, t_len, n_kv, HEAD_DIM))


def _group_q(q, n_kv):
    bsz, t_len, h, d = q.shape
    return q.reshape(bsz, t_len, n_kv, h // n_kv, d)


def _flat(a):
    return a.reshape(a.shape[0], a.shape[1], -1)


def _mla_queries(p, q_norm, w_uq, rope):
    bsz, t_len, _ = p.shape
    q = _linear(_rms_norm(p[..., :MLA_Q_RANK], q_norm), w_uq).reshape(bsz, t_len, MLA_HEADS, MLA_NOPE + MLA_ROPE)
    if rope is not None:
        q = jnp.concatenate([q[..., :MLA_NOPE], _apply_rope(q[..., MLA_NOPE:], *rope)], axis=-1)
    pad = jnp.zeros((bsz, t_len, MLA_HEADS, MLA_QK_PAD - MLA_NOPE - MLA_ROPE), q.dtype)
    return _flat(jnp.concatenate([q, pad], axis=-1))


def _mla_keys_values(p, kv_norm, w_ukv, rope):
    bsz, t_len, _ = p.shape
    ckv = p[..., MLA_Q_RANK:MLA_Q_RANK + MLA_KV_RANK]
    k_rope = p[..., MLA_Q_RANK + MLA_KV_RANK:][:, :, None, :]
    if rope is not None:
        k_rope = _apply_rope(k_rope, *rope)
    kv = _linear(_rms_norm(ckv, kv_norm), w_ukv).reshape(bsz, t_len, MLA_HEADS, MLA_NOPE + MLA_V)
    pad = jnp.zeros((bsz, t_len, MLA_HEADS, MLA_QK_PAD - MLA_NOPE - MLA_ROPE), kv.dtype)
    k = jnp.concatenate([kv[..., :MLA_NOPE], jnp.broadcast_to(k_rope, (bsz, t_len, MLA_HEADS, MLA_ROPE)), pad], axis=-1)
    return _flat(k).astype(BF16), _flat(kv[..., MLA_NOPE:]).astype(BF16)


def _mla_mixer(px, pc, q_norm, w_uq, kv_norm, w_ukv, rope, ctx_out):
    scale = (MLA_NOPE + MLA_ROPE) ** -0.5
    attn = functools.partial(_dense_attention, kv_heads=MLA_HEADS, groups=1, dk=MLA_QK_PAD, dv=MLA_V, scale=scale)
    kx, vx = _mla_keys_values(px, kv_norm, w_ukv, rope)
    kc, vc = _mla_keys_values(pc, kv_norm, w_ukv, None)
    qx = _mla_queries(px, q_norm, w_uq, rope)
    y_x = attn(qx, jnp.concatenate([kc, kx], axis=1), jnp.concatenate([vc, vx], axis=1))
    y_c = attn(_mla_queries(pc, q_norm, w_uq, None), kc, vc) if ctx_out else None
    return y_x, y_c


def _gqa_mixer(px, pc, q_norm, k_norm, rope, ctx_out):
    scale = HEAD_DIM ** -0.5
    attn = functools.partial(_dense_attention, kv_heads=GQA_KV_HEADS, groups=GQA_HEADS // GQA_KV_HEADS, dk=HEAD_DIM,
                             dv=HEAD_DIM, scale=scale)
    qx, kx, vx = _attn_heads(px, GQA_HEADS, GQA_KV_HEADS)
    qc, kc, vc = _attn_heads(pc, GQA_HEADS, GQA_KV_HEADS)
    qx = _apply_rope(_rms_norm(qx, q_norm), *rope)
    kx = _apply_rope(_rms_norm(kx, k_norm), *rope)
    kc = _rms_norm(kc, k_norm)
    k_all = _flat(jnp.concatenate([kc, kx], axis=1)).astype(BF16)
    v_all = _flat(jnp.concatenate([vc, vx], axis=1)).astype(BF16)
    y_x = attn(_flat(qx), k_all, v_all)
    y_c = attn(_flat(_rms_norm(qc, q_norm)), _flat(kc).astype(BF16), _flat(vc).astype(BF16)) if ctx_out else None
    return y_x, y_c


def _window_attention(q, k, v, kc, vc, sink, scale):
    bsz, s_len, hk, g, d = q.shape
    nb = s_len // WINDOW
    qb = q.reshape(bsz, nb, WINDOW, hk, g, d)
    pad = ((0, 0), (WINDOW, WINDOW), (0, 0), (0, 0))
    kp = jnp.pad(k, pad).reshape(bsz, nb + 2, WINDOW, hk, d)
    vp = jnp.pad(v, pad).reshape(bsz, nb + 2, WINDOW, hk, d)
    kband = jnp.concatenate([kp[:, :-2], kp[:, 1:-1], kp[:, 2:]], axis=2)
    vband = jnp.concatenate([vp[:, :-2], vp[:, 1:-1], vp[:, 2:]], axis=2)
    s_win = jnp.einsum('bnqhgd,bnkhd->bnhgqk', qb, kband).astype(F32) * scale
    blk = jnp.arange(nb)[:, None] * WINDOW
    q_pos = blk + jnp.arange(WINDOW)[None, :]
    k_pos = blk - WINDOW + jnp.arange(3 * WINDOW)[None, :]
    valid = (jnp.abs(k_pos[:, None, :] - q_pos[:, :, None]) <= WINDOW) & (k_pos >= 0)[:, None, :] & (k_pos < s_len)[:, None, :]
    s_win = jnp.where(valid[None, :, None, None], s_win, -jnp.inf)
    s_ctx = jnp.einsum('bnqhgd,bchd->bnhgqc', qb, kc).astype(F32) * scale
    s_sink = jnp.broadcast_to(sink.reshape(hk, g)[None, None, :, :, None, None].astype(F32), s_ctx.shape[:-1] + (1,))
    p = jax.nn.softmax(jnp.concatenate([s_ctx, s_win, s_sink], axis=-1), axis=-1).astype(v.dtype)
    n_ctx = kc.shape[1]
    o = jnp.einsum('bnhgqc,bchd->bnqhgd', p[..., :n_ctx], vc) + jnp.einsum('bnhgqk,bnkhd->bnqhgd', p[..., n_ctx:n_ctx + 3 * WINDOW], vband)
    return o.reshape(bsz, s_len, hk * g * d)


def _sink_attention(q, k, v, sink, scale):
    bsz, t_len, hk, g, d = q.shape
    s = jnp.einsum('bqhgd,bkhd->bhgqk', q, k).astype(F32) * scale
    s_sink = jnp.broadcast_to(sink.reshape(hk, g)[None, :, :, None, None].astype(F32), s.shape[:-1] + (1,))
    p = jax.nn.softmax(jnp.concatenate([s, s_sink], axis=-1), axis=-1)[..., :-1].astype(v.dtype)
    return jnp.einsum('bhgqk,bkhd->bqhgd', p, v).reshape(bsz, t_len, hk * g * d)


def _swa_mixer(px, pc, sink, rope, ctx_out):
    scale = HEAD_DIM ** -0.5
    qx, kx, vx = _attn_heads(px, SWA_HEADS, SWA_KV_HEADS)
    qc, kc, vc = _attn_heads(pc, SWA_HEADS, SWA_KV_HEADS)
    qx = _apply_rope(qx, *rope)
    kx = _apply_rope(kx, *rope)
    y_x = _window_attention(_group_q(qx, SWA_KV_HEADS), kx, vx, kc, vc, sink, scale)
    y_c = _sink_attention(_group_q(qc, SWA_KV_HEADS), kc, vc, sink, scale) if ctx_out else None
    return y_x, y_c


def _centred_dwconv(x, w, b):
    y = lax.conv_general_dilated(x, w[:, None, :], window_strides=(1,), padding=((SSD_CONV // 2, SSD_CONV // 2),),
                                 dimension_numbers=('NWC', 'WIO', 'NWC'), feature_group_count=x.shape[-1])
    return y + b


def _ssd_chunked(x, dt, A, Bm, Cm, h0, with_output):
    bsz, t_len = x.shape[:2]
    L, G, Hg, P, N = SSD_CHUNK, SSD_GROUPS, SSD_HEADS // SSD_GROUPS, SSD_HEAD_DIM, SSD_STATE
    nc = t_len // L
    xc = x.astype(F32).reshape(bsz, nc, L, G, Hg, P)
    dtc = dt.astype(F32).reshape(bsz, nc, L, G, Hg)
    Bc = Bm.astype(F32).reshape(bsz, nc, L, G, N)
    Cc = Cm.astype(F32).reshape(bsz, nc, L, G, N)
    a_cum = jnp.cumsum(dtc * A.astype(F32).reshape(G, Hg), axis=2)
    a_end = a_cum[:, :, -1]
    states = jnp.einsum('bclgn,bclgh,bclghp->bcghpn', Bc, jnp.exp(a_end[:, :, None] - a_cum) * dtc, xc)

    def carry_state(h, inp):
        decay, st = inp
        return h * decay[..., None, None] + st, h

    h_final, h_enter = lax.scan(carry_state, h0.astype(F32), (jnp.moveaxis(jnp.exp(a_end), 1, 0), jnp.moveaxis(states, 1, 0)))
    if not with_output:
        return None, h_final
    h_enter = jnp.moveaxis(h_enter, 0, 1)
    causal = jnp.tril(jnp.ones((L, L), dtype=bool))
    seg = a_cum[:, :, :, None] - a_cum[:, :, None]
    decay = jnp.exp(jnp.where(causal[:, :, None, None], seg, -jnp.inf))
    cb = jnp.einsum('bclgn,bcsgn->bclsg', Cc, Bc)
    y_diag = jnp.einsum('bclsgh,bcsghp->bclghp', cb[..., None] * decay * dtc[:, :, None], xc)
    y_off = jnp.einsum('bclgn,bcghpn->bclghp', Cc, h_enter) * jnp.exp(a_cum)[..., None]
    return (y_diag + y_off).reshape(bsz, t_len, SSD_HEADS, P), h_final


def _ssd_mixer(px, pc, conv_w, conv_b, a_log, dt_bias, d_skip, norm_g, ctx_out):
    def prep(p):
        bsz, t_len, _ = p.shape
        z, xbc, dt = jnp.split(p, [SSD_INNER, SSD_INNER + SSD_CONV_DIM], axis=-1)
        xbc = jax.nn.silu(_centred_dwconv(xbc, conv_w, conv_b))
        xs, Bm, Cm = jnp.split(xbc, [SSD_INNER, SSD_INNER + SSD_GROUPS * SSD_STATE], axis=-1)
        xs = xs.reshape(bsz, t_len, SSD_HEADS, SSD_HEAD_DIM)
        Bm = Bm.reshape(bsz, t_len, SSD_GROUPS, SSD_STATE)
        Cm = Cm.reshape(bsz, t_len, SSD_GROUPS, SSD_STATE)
        dt = jax.nn.softplus(dt.reshape(bsz, t_len, 2, SSD_HEADS) + dt_bias)
        return z, xs, Bm, Cm, dt

    def flip(a):
        return jnp.flip(a, axis=1)

    A = -jnp.exp(a_log.astype(F32))
    zx, xx, Bx, Cx, dtx = prep(px)
    zc, xc, Bc, Cc, dtc = prep(pc)
    h0 = jnp.zeros((px.shape[0], SSD_GROUPS, SSD_HEADS // SSD_GROUPS, SSD_HEAD_DIM, SSD_STATE), F32)
    yc_f, hc_f = _ssd_chunked(xc, dtc[:, :, 0], A[0], Bc, Cc, h0, ctx_out)
    yx_f, _ = _ssd_chunked(xx, dtx[:, :, 0], A[0], Bx, Cx, hc_f, True)
    yc_b, hc_b = _ssd_chunked(flip(xc), flip(dtc[:, :, 1]), A[1], flip(Bc), flip(Cc), h0, ctx_out)
    yx_b, _ = _ssd_chunked(flip(xx), flip(dtx[:, :, 1]), A[1], flip(Bx), flip(Cx), hc_b, True)

    def finish(y_f, y_b, xs, z):
        y = (y_f + flip(y_b) + d_skip.astype(F32)[:, None] * xs.astype(F32)).astype(xs.dtype)
        bsz, t_len = xs.shape[:2]
        return _rms_norm(y.reshape(bsz, t_len, SSD_INNER) * jax.nn.silu(z), norm_g)

    y_x = finish(yx_f, yx_b, xx, zx)
    y_c = finish(yc_f, yc_b, xc, zc) if ctx_out else None
    return y_x, y_c


PEER_SEL = PEER_HEADS * PEER_TOPK
PEER_HALF = PEER_DKEY // 2
PEER_ROUTE_TILE = 128
PEER_APPLY_TILE = 128
PEER_GROUP = 8
PEER_SLOTS = 8
PEER_LOOKAHEAD = 6
PEER_CHUNK = 256


def _top_extract(vals, iota, count, payload=None):
    n_rows = vals.shape[0]
    top_v, top_i = [], []
    for _ in range(count):
        m = jnp.max(vals, axis=0, keepdims=True)
        idx = jnp.min(jnp.where(vals == m, iota, n_rows), axis=0, keepdims=True)
        sel = iota == idx
        top_v.append(m)
        top_i.append(idx if payload is None else jnp.sum(jnp.where(sel, payload, 0), axis=0, keepdims=True))
        vals = jnp.where(sel, -jnp.inf, vals)
    return jnp.concatenate(top_v, axis=0), jnp.concatenate(top_i, axis=0)


def _peer_route_kernel(q_ref, keys_ref, e_ref, g_ref, et_ref, gt_ref):
    t_len = q_ref.shape[0]
    iota_k = lax.broadcasted_iota(jnp.int32, (PEER_KEYS, t_len), 0)
    iota_c = lax.broadcasted_iota(jnp.int32, (PEER_TOPK * PEER_TOPK, t_len), 0)

    def head(h, carry):
        tops = []
        for p in range(2):
            col = pl.multiple_of((2 * h + p) * PEER_HALF, PEER_HALF)
            qs = q_ref[:, pl.ds(col, PEER_HALF)].astype(BF16)
            s_t = lax.dot_general(keys_ref[2 * h + p], qs, (((1,), (1,)), ((), ())), preferred_element_type=F32)
            tops.append(_top_extract(s_t, iota_k, PEER_TOPK))
        (s1, i1), (s2, i2) = tops
        cand = jnp.concatenate([s1[a:a + 1] + s2 for a in range(PEER_TOPK)], axis=0)
        expert = jnp.concatenate([i1[a:a + 1] * PEER_KEYS + i2 for a in range(PEER_TOPK)], axis=0)
        sc, e_h = _top_extract(cand, iota_c, PEER_TOPK, payload=expert)
        p_h = jnp.exp(sc - sc[0:1])
        row = pl.multiple_of(h * PEER_TOPK, PEER_TOPK)
        gt_ref[pl.ds(row, PEER_TOPK), :] = p_h / jnp.sum(p_h, axis=0, keepdims=True)
        et_ref[pl.ds(row, PEER_TOPK), :] = e_h
        return carry

    lax.fori_loop(0, PEER_HEADS, head, 0)
    e_ref[...] = et_ref[...].T
    g_ref[...] = gt_ref[...].T


def _peer_route(q, sub_keys):
    t_len = q.shape[0]
    tile = _largest_tile(t_len, PEER_ROUTE_TILE, LANES)
    keys = sub_keys.reshape(PEER_HEADS * 2, PEER_KEYS, PEER_HALF).astype(BF16)
    return pl.pallas_call(
        _peer_route_kernel,
        name="peer_route",
        grid=(t_len // tile,),
        in_specs=[pl.BlockSpec((tile, PEER_HEADS * PEER_DKEY), lambda i: (i, 0)),
                  pl.BlockSpec((PEER_HEADS * 2, PEER_KEYS, PEER_HALF), lambda i: (0, 0, 0))],
        out_specs=[pl.BlockSpec((tile, PEER_SEL), lambda i: (i, 0)),
                   pl.BlockSpec((tile, PEER_SEL), lambda i: (i, 0))],
        out_shape=[jax.ShapeDtypeStruct((t_len, PEER_SEL), jnp.int32), jax.ShapeDtypeStruct((t_len, PEER_SEL), F32)],
        scratch_shapes=[pltpu.VMEM((PEER_SEL, tile), jnp.int32), pltpu.VMEM((PEER_SEL, tile), F32)],
        compiler_params=pltpu.CompilerParams(dimension_semantics=("arbitrary",), vmem_limit_bytes=VMEM_LIMIT_BYTES),
    )(q, keys)


def _pack_peer_table(u, v):
    ub = lax.bitcast_convert_type(u.astype(jnp.bfloat16), jnp.uint16).astype(jnp.uint32)
    vb = lax.bitcast_convert_type(v.astype(jnp.bfloat16), jnp.uint16).astype(jnp.uint32)
    return (ub << 16) | vb


def _peer_apply_kernel(e_ref, h_ref, sh_ref, sc_ref, g_ref, gate_ref, lng_ref, lnb_ref, tab_ref, o_ref, buf, sem, *,
                       alpha):
    t_len = h_ref.shape[1]
    n_groups = t_len // PEER_GROUP

    def row_copy(tok, j, slot):
        return pltpu.make_async_copy(tab_ref.at[pl.ds(e_ref[tok, j], 1)], buf.at[slot, pl.ds(j, 1)], sem.at[slot])

    d = h_ref.shape[2]
    chunk = min(d, PEER_CHUNK)
    n_chunks = d // chunk
    per_chunk = PEER_SEL // (2 * n_chunks)

    def request(tok, slot, rows):
        for j in rows:
            row_copy(tok, j, slot).start(priority=j % 2)

    for tok in range(PEER_LOOKAHEAD):
        request(tok, tok % PEER_SLOTS, range(PEER_SEL))

    row_id = lax.broadcasted_iota(jnp.int32, (PEER_GROUP, PEER_SEL), 0)
    shift, scale, gate = sh_ref[0], sc_ref[0], gate_ref[0]
    ln_g, ln_b = lng_ref[...], lnb_ref[...]

    def group(base, n_requests):
        h8 = h_ref[0, pl.ds(base, PEER_GROUP), :]
        hm8 = (h8 * (1.0 + scale) + shift).astype(BF16)
        g8 = g_ref[pl.ds(base, PEER_GROUP), :]
        ffn = [jnp.zeros((PEER_GROUP, chunk), F32) for _ in range(n_chunks)]
        for r in range(PEER_GROUP):
            pltpu.make_async_copy(tab_ref.at[pl.ds(0, PEER_SEL)], buf.at[r], sem.at[r]).wait()

            def interleaved_requests(step):
                if r < n_requests:
                    request(base + r + PEER_LOOKAHEAD, (r + PEER_LOOKAHEAD) % PEER_SLOTS,
                            range(step * per_chunk, (step + 1) * per_chunk))

            s = jnp.zeros((PEER_GROUP, PEER_SEL), F32)
            for c in range(n_chunks):
                interleaved_requests(c)
                words = buf[r, :, c * chunk:(c + 1) * chunk]
                u_rows = lax.bitcast_convert_type(words & jnp.uint32(0xFFFF0000), F32).astype(BF16)
                s = s + lax.dot_general(hm8[:, c * chunk:(c + 1) * chunk], u_rows, (((1,), (1,)), ((), ())),
                                        preferred_element_type=F32)
            w = jnp.where(row_id == r, g8 * jax.nn.gelu(s), 0.0).astype(BF16)
            for c in range(n_chunks):
                interleaved_requests(n_chunks + c)
                words = buf[r, :, c * chunk:(c + 1) * chunk]
                v_rows = lax.bitcast_convert_type(words << 16, F32).astype(BF16)
                ffn[c] = ffn[c] + jnp.dot(w, v_rows, preferred_element_type=F32)
        y = alpha * h8 + gate * jnp.concatenate(ffn, axis=1)
        mu = jnp.mean(y, axis=-1, keepdims=True)
        var = jnp.mean(jnp.square(y - mu), axis=-1, keepdims=True)
        o_ref[0, pl.ds(base, PEER_GROUP), :] = (y - mu) * lax.rsqrt(var + EPS) * ln_g + ln_b

    def full_group(gi, carry):
        group(pl.multiple_of(gi * PEER_GROUP, PEER_GROUP), PEER_GROUP)
        return carry

    lax.fori_loop(0, n_groups - 1, full_group, 0)
    group((n_groups - 1) * PEER_GROUP, PEER_GROUP - PEER_LOOKAHEAD)


def _peer_block(h, shift, scale, gate, w_q, sub_keys, table, ln_g, ln_b, alpha):
    bsz, t_len, d = h.shape
    q = _linear(h, w_q, shift, scale)
    e, g = _peer_route(q.reshape(bsz * t_len, -1), sub_keys)
    tile = _largest_tile(t_len, PEER_APPLY_TILE, PEER_GROUP)
    n_t = t_len // tile
    row_spec = pl.BlockSpec((1, 1, d), lambda b, i: (b, 0, 0))
    vec_spec = pl.BlockSpec((1, d), lambda b, i: (0, 0))
    return pl.pallas_call(
        functools.partial(_peer_apply_kernel, alpha=alpha),
        name="peer_apply",
        grid=(bsz, n_t),
        in_specs=[
            pl.BlockSpec((tile, PEER_SEL), lambda b, i: (b * n_t + i, 0), memory_space=pltpu.SMEM),
            pl.BlockSpec((1, tile, d), lambda b, i: (b, i, 0)),
            row_spec, row_spec,
            pl.BlockSpec((tile, PEER_SEL), lambda b, i: (b * n_t + i, 0)),
            row_spec, vec_spec, vec_spec,
            pl.BlockSpec(memory_space=pl.ANY),
        ],
        out_specs=pl.BlockSpec((1, tile, d), lambda b, i: (b, i, 0)),
        out_shape=jax.ShapeDtypeStruct((bsz, t_len, d), F32),
        scratch_shapes=[pltpu.VMEM((PEER_SLOTS, PEER_SEL, d), jnp.uint32), pltpu.SemaphoreType.DMA((PEER_SLOTS,))],
        compiler_params=pltpu.CompilerParams(dimension_semantics=("arbitrary", "arbitrary"),
                                             vmem_limit_bytes=VMEM_LIMIT_BYTES),
    )(e, h, shift, scale, g, gate, ln_g.reshape(1, d), ln_b.reshape(1, d), table)


def kernel(x, c, ctx, c_ctx, w_ada, b_ada, w_in, mla_q_norm, mla_w_uq, mla_kv_norm, mla_w_ukv, gqa_q_norm, gqa_k_norm, ssd_conv_w, ssd_conv_b, ssd_a_log, ssd_dt_bias, ssd_d, ssd_norm, swa_sink, w_out, ln1_g, ln1_b, peer_w_q, peer_sub_keys, peer_u, peer_v, ln2_g, ln2_b):
    bsz = x.shape[0]
    depth = w_in.shape[0]
    rows = x.shape[1] // GRID_W
    rope_mla = _axial_rope_tables(rows, MLA_ROPE)
    rope_head = _axial_rope_tables(rows, HEAD_DIM)
    alpha = (2.0 * depth) ** 0.25
    silu_c = jax.nn.silu(c)[:, None, :]
    silu_cc = jax.nn.silu(c_ctx)[None, None, :]
    for l in range(depth):
        ctx_out = l < depth - 1
        mx = jnp.split(silu_c @ w_ada[l] + b_ada[l], 6, axis=-1)
        mc = [jnp.broadcast_to(m, (bsz,) + m.shape[1:]) for m in jnp.split(silu_cc @ w_ada[l] + b_ada[l], 6, axis=-1)]
        w_in_l = w_in[l].astype(BF16)
        px = jnp.split(_linear(x, w_in_l, mx[0], mx[1]), GROUP_SPLITS, axis=-1)
        pc = jnp.split(_linear(ctx, w_in_l, mc[0], mc[1]), GROUP_SPLITS, axis=-1)
        w_uq = mla_w_uq[l].astype(BF16)
        w_ukv = mla_w_ukv[l].astype(BF16)
        ya_x, ya_c = _mla_mixer(px[0], pc[0], mla_q_norm[l], w_uq, mla_kv_norm[l], w_ukv, rope_mla, ctx_out)
        yb_x, yb_c = _gqa_mixer(px[1], pc[1], gqa_q_norm[l], gqa_k_norm[l], rope_head, ctx_out)
        yc_x, yc_c = _ssd_mixer(px[2], pc[2], ssd_conv_w[l], ssd_conv_b[l], ssd_a_log[l], ssd_dt_bias[l], ssd_d[l], ssd_norm[l], ctx_out)
        yd_x, yd_c = _swa_mixer(px[3], pc[3], swa_sink[l], rope_head, ctx_out)
        w_out_l = w_out[l].astype(BF16)
        w_pq = peer_w_q[l].astype(BF16)
        mix_x = _linear(jnp.concatenate([ya_x, yb_x, yc_x, yd_x], axis=-1), w_out_l)
        table = _pack_peer_table(peer_u[l], peer_v[l])
        x = _layer_norm(alpha * x + mx[2] * mix_x, ln1_g[l], ln1_b[l])
        x = _peer_block(x, mx[3], mx[4], mx[5], w_pq, peer_sub_keys[l], table, ln2_g[l], ln2_b[l], alpha)
        if ctx_out:
            mix_c = _linear(jnp.concatenate([ya_c, yb_c, yc_c, yd_c], axis=-1), w_out_l)
            ctx = _layer_norm(alpha * ctx + mc[2] * mix_c, ln1_g[l], ln1_b[l])
            ctx = _peer_block(ctx, mc[3], mc[4], mc[5], w_pq, peer_sub_keys[l], table, ln2_g[l], ln2_b[l], alpha)
    return x
```

```python
import functools
import math

import jax
import jax.numpy as jnp
from jax import lax
from jax.experimental import pallas as pl
from jax.experimental.pallas import tpu as pltpu

GRID_W = 64
ROPE_BASE = 10000.0
EPS = 1e-6
HEAD_DIM = 128

MLA_HEADS = 4
MLA_Q_RANK = 512
MLA_KV_RANK = 256
MLA_NOPE = 128
MLA_ROPE = 64
MLA_V = 128
MLA_QK_PAD = 256

GQA_HEADS = 4
GQA_KV_HEADS = 2

SSD_HEADS = 8
SSD_HEAD_DIM = 64
SSD_INNER = SSD_HEADS * SSD_HEAD_DIM
SSD_GROUPS = 2
SSD_STATE = 128
SSD_CONV = 5
SSD_CHUNK = 128
SSD_CONV_DIM = SSD_INNER + 2 * SSD_GROUPS * SSD_STATE

SWA_HEADS = 4
SWA_KV_HEADS = 2
WINDOW = 128

PEER_HEADS = 8
PEER_KEYS = 128
PEER_TOPK = 16
PEER_DKEY = 256
PEER_BLOCK = 128

MLA_COLS = MLA_Q_RANK + MLA_KV_RANK + MLA_ROPE
GQA_COLS = (GQA_HEADS + 2 * GQA_KV_HEADS) * HEAD_DIM
SSD_COLS = SSD_INNER + SSD_CONV_DIM + 2 * SSD_HEADS
SWA_COLS = (SWA_HEADS + 2 * SWA_KV_HEADS) * HEAD_DIM
GROUP_SPLITS = [MLA_COLS, MLA_COLS + GQA_COLS, MLA_COLS + GQA_COLS + SSD_COLS]

VMEM_LIMIT_BYTES = 56 * 1024 * 1024
LANES = 128

F32 = jnp.float32
BF16 = jnp.bfloat16


def _largest_tile(n, cap, unit):
    return max(t for t in range(unit, min(n, cap) + 1, unit) if n % t == 0)


def _linear_mod_kernel(x_ref, sh_ref, sc_ref, w_ref, o_ref):
    xm = x_ref[0] * (1.0 + sc_ref[0]) + sh_ref[0]
    o_ref[0] = jnp.dot(xm.astype(BF16), w_ref[...], preferred_element_type=F32)


def _linear_kernel(x_ref, w_ref, o_ref):
    o_ref[0] = jnp.dot(x_ref[0].astype(BF16), w_ref[...], preferred_element_type=F32)


def _linear(x, w, shift=None, scale=None, tm=512):
    bsz, t_len, k_dim = x.shape
    n_dim = w.shape[1]
    n_pad = -(-n_dim // LANES) * LANES
    if n_pad != n_dim:
        w = jnp.pad(w, ((0, 0), (0, n_pad - n_dim)))
    tn = _largest_tile(n_pad, 1024, LANES)
    tm = _largest_tile(t_len, tm, 8)
    grid = (bsz, t_len // tm, n_pad // tn)
    x_spec = pl.BlockSpec((1, tm, k_dim), lambda b, i, j: (b, i, 0))
    w_spec = pl.BlockSpec((k_dim, tn), lambda b, i, j: (0, j))
    o_spec = pl.BlockSpec((1, tm, tn), lambda b, i, j: (b, i, j))
    params = pltpu.CompilerParams(dimension_semantics=("arbitrary", "arbitrary", "arbitrary"),
                                  vmem_limit_bytes=VMEM_LIMIT_BYTES)
    out_shape = jax.ShapeDtypeStruct((bsz, t_len, n_pad), F32)
    if shift is None:
        out = pl.pallas_call(_linear_kernel, grid=grid, in_specs=[x_spec, w_spec], out_specs=o_spec,
                             out_shape=out_shape, compiler_params=params, name="linear")(x, w)
    else:
        m_spec = pl.BlockSpec((1, 1, k_dim), lambda b, i, j: (b, 0, 0))
        out = pl.pallas_call(_linear_mod_kernel, grid=grid, in_specs=[x_spec, m_spec, m_spec, w_spec], out_specs=o_spec,
                             out_shape=out_shape, compiler_params=params, name="linear_mod")(x, shift, scale, w)
    return out if n_pad == n_dim else out[..., :n_dim]


def _linear_residual_ln_kernel(a_ref, w_ref, x_ref, gate_ref, g_ref, b_ref, o_ref, *, alpha):
    y = alpha * x_ref[0] + gate_ref[0] * jnp.dot(a_ref[0].astype(BF16), w_ref[...], preferred_element_type=F32)
    mu = jnp.mean(y, axis=-1, keepdims=True)
    var = jnp.mean(jnp.square(y - mu), axis=-1, keepdims=True)
    o_ref[0] = (y - mu) * lax.rsqrt(var + EPS) * g_ref[...] + b_ref[...]


def _linear_residual_ln(a, w, x, gate, ln_g, ln_b, alpha, tm=256):
    bsz, t_len, k_dim = a.shape
    d = w.shape[1]
    tm = _largest_tile(t_len, tm, 8)
    row = pl.BlockSpec((1, tm, d), lambda b, i: (b, i, 0))
    vec = pl.BlockSpec((1, d), lambda b, i: (0, 0))
    return pl.pallas_call(
        functools.partial(_linear_residual_ln_kernel, alpha=alpha),
        name="linear_residual_ln",
        grid=(bsz, t_len // tm),
        in_specs=[pl.BlockSpec((1, tm, k_dim), lambda b, i: (b, i, 0)), pl.BlockSpec((k_dim, d), lambda b, i: (0, 0)),
                  row, pl.BlockSpec((1, 1, d), lambda b, i: (b, 0, 0)), vec, vec],
        out_specs=row,
        out_shape=jax.ShapeDtypeStruct((bsz, t_len, d), F32),
        compiler_params=pltpu.CompilerParams(dimension_semantics=("arbitrary", "arbitrary"),
                                             vmem_limit_bytes=VMEM_LIMIT_BYTES),
    )(a, w, x, gate, ln_g.reshape(1, d), ln_b.reshape(1, d))


def _dense_attn_kernel(q_ref, k_ref, v_ref, o_ref, s_even, s_odd, m_ref, acc_ref, *, groups, dk, dv, tk, scale):
    tq = q_ref.shape[1]
    n_chunks = k_ref.shape[1] // tk
    q = q_ref[0] * scale
    q = jnp.concatenate([q[:, g * dk:(g + 1) * dk] for g in range(groups)], axis=0).astype(BF16)
    m_ref[...] = jnp.full(m_ref.shape, -jnp.inf, F32)
    acc_ref[...] = jnp.zeros(acc_ref.shape, F32)

    def scores(j, s_ref):
        kb = k_ref[0, pl.ds(pl.multiple_of(j * tk, tk), tk), :]
        s_ref[...] = lax.dot_general(q, kb, (((1,), (1,)), ((), ())), preferred_element_type=F32)

    def accumulate(j, s_ref):
        vb = v_ref[0, pl.ds(pl.multiple_of(j * tk, tk), tk), :]
        s = s_ref[...]
        m_old = m_ref[...]
        m_new = jnp.maximum(m_old, jnp.max(s, axis=-1, keepdims=True))
        p = jnp.exp(s - m_new).astype(BF16)
        acc_ref[...] = jnp.exp(m_old - m_new) * acc_ref[...] + jnp.dot(p, vb, preferred_element_type=F32)
        m_ref[...] = m_new

    n_pairs = (n_chunks - 1) // 2
    scores(0, s_even)

    def pair(i, carry):
        scores(2 * i + 1, s_odd)
        accumulate(2 * i, s_even)
        scores(2 * i + 2, s_even)
        accumulate(2 * i + 1, s_odd)
        return carry

    lax.fori_loop(0, n_pairs, pair, 0)
    if n_chunks % 2 == 0:
        scores(n_chunks - 1, s_odd)
        accumulate(n_chunks - 2, s_even)
        accumulate(n_chunks - 1, s_odd)
    else:
        accumulate(n_chunks - 1, s_even)
    acc = acc_ref[...]
    o = acc[:, :dv] / acc[:, dv:dv + 1]
    o_ref[0] = jnp.concatenate([o[g * tq:(g + 1) * tq] for g in range(groups)], axis=1)


def _dense_attention(q, k, v, *, kv_heads, groups, dk, dv, scale, rows=512, tk=1280):
    bsz, s_len, _ = q.shape
    sk = k.shape[1]
    tq = _largest_tile(s_len, rows // groups, 8)
    tk = _largest_tile(sk, tk, LANES)
    dvp = dv + LANES
    v4 = v.reshape(bsz, sk, kv_heads, dv)
    v_ext = jnp.concatenate([v4, jnp.ones((bsz, sk, kv_heads, 1), v.dtype),
                             jnp.zeros((bsz, sk, kv_heads, LANES - 1), v.dtype)], axis=-1).reshape(bsz, sk, kv_heads * dvp)
    kern = functools.partial(_dense_attn_kernel, groups=groups, dk=dk, dv=dv, tk=tk, scale=scale)
    return pl.pallas_call(
        kern,
        name="dense_attn",
        grid=(bsz, kv_heads, s_len // tq),
        in_specs=[
            pl.BlockSpec((1, tq, groups * dk), lambda b, h, i: (b, i, h)),
            pl.BlockSpec((1, sk, dk), lambda b, h, i: (b, 0, h)),
            pl.BlockSpec((1, sk, dvp), lambda b, h, i: (b, 0, h)),
        ],
        out_specs=pl.BlockSpec((1, tq, groups * dv), lambda b, h, i: (b, i, h)),
        out_shape=jax.ShapeDtypeStruct((bsz, s_len, kv_heads * groups * dv), F32),
        scratch_shapes=[
            pltpu.VMEM((groups * tq, tk), F32),
            pltpu.VMEM((groups * tq, tk), F32),
            pltpu.VMEM((groups * tq, 1), F32),
            pltpu.VMEM((groups * tq, dvp), F32),
        ],
        compiler_params=pltpu.CompilerParams(dimension_semantics=("arbitrary", "arbitrary", "arbitrary"),
                                             vmem_limit_bytes=VMEM_LIMIT_BYTES),
    )(q, k, v_ext)


def _rms_norm(x, g):
    xf = x.astype(F32)
    y = xf * lax.rsqrt(jnp.mean(xf * xf, axis=-1, keepdims=True) + EPS)
    return (y * g.astype(F32)).astype(x.dtype)


def _layer_norm(x, g, b):
    xf = x.astype(F32)
    mu = jnp.mean(xf, axis=-1, keepdims=True)
    var = jnp.mean(jnp.square(xf - mu), axis=-1, keepdims=True)
    y = (xf - mu) * lax.rsqrt(var + EPS) * g.astype(F32) + b.astype(F32)
    return y.astype(x.dtype)


def _modulate(x, shift, scale):
    return x * (1 + scale) + shift


def _axial_rope_tables(rows, rot_dim):
    row = jnp.repeat(jnp.arange(rows, dtype=F32), GRID_W)
    col = jnp.tile(jnp.arange(GRID_W, dtype=F32), rows)
    n_freq = rot_dim // 4
    inv_freq = ROPE_BASE ** (-jnp.arange(n_freq, dtype=F32) / n_freq)
    ang = jnp.concatenate([row[:, None] * inv_freq, col[:, None] * inv_freq], axis=-1)
    return jnp.cos(ang), jnp.sin(ang)


def _apply_rope(x, cos, sin):
    half = x.shape[-1] // 2
    x1, x2 = x[..., :half], x[..., half:]
    c = cos[None, :, None, :].astype(x.dtype)
    s = sin[None, :, None, :].astype(x.dtype)
    return jnp.concatenate([x1 * c - x2 * s, x1 * s + x2 * c], axis=-1)


def _attn_heads(p, n_q, n_kv):
    bsz, t_len, _ = p.shape
    q, k, v = jnp.split(p, [n_q * HEAD_DIM, (n_q + n_kv) * HEAD_DIM], axis=-1)
    return (q.reshape(bsz, t_len, n_q, HEAD_DIM), k.reshape(bsz, t_len, n_kv, HEAD_DIM), v.reshape(bsz, t_len, n_kv, HEAD_DIM))


def _group_q(q, n_kv):
    bsz, t_len, h, d = q.shape
    return q.reshape(bsz, t_len, n_kv, h // n_kv, d)


def _flat(a):
    return a.reshape(a.shape[0], a.shape[1], -1)


def _mla_queries(p, q_norm, w_uq, rope):
    bsz, t_len, _ = p.shape
    q = _linear(_rms_norm(p[..., :MLA_Q_RANK], q_norm), w_uq).reshape(bsz, t_len, MLA_HEADS, MLA_NOPE + MLA_ROPE)
    if rope is not None:
        q = jnp.concatenate([q[..., :MLA_NOPE], _apply_rope(q[..., MLA_NOPE:], *rope)], axis=-1)
    pad = jnp.zeros((bsz, t_len, MLA_HEADS, MLA_QK_PAD - MLA_NOPE - MLA_ROPE), q.dtype)
    return _flat(jnp.concatenate([q, pad], axis=-1))


def _mla_keys_values(p, kv_norm, w_ukv, rope):
    bsz, t_len, _ = p.shape
    ckv = p[..., MLA_Q_RANK:MLA_Q_RANK + MLA_KV_RANK]
    k_rope = p[..., MLA_Q_RANK + MLA_KV_RANK:][:, :, None, :]
    if rope is not None:
        k_rope = _apply_rope(k_rope, *rope)
    kv = _linear(_rms_norm(ckv, kv_norm), w_ukv).reshape(bsz, t_len, MLA_HEADS, MLA_NOPE + MLA_V)
    pad = jnp.zeros((bsz, t_len, MLA_HEADS, MLA_QK_PAD - MLA_NOPE - MLA_ROPE), kv.dtype)
    k = jnp.concatenate([kv[..., :MLA_NOPE], jnp.broadcast_to(k_rope, (bsz, t_len, MLA_HEADS, MLA_ROPE)), pad], axis=-1)
    return _flat(k).astype(BF16), _flat(kv[..., MLA_NOPE:]).astype(BF16)


def _mla_mixer(px, pc, q_norm, w_uq, kv_norm, w_ukv, rope, ctx_out):
    scale = (MLA_NOPE + MLA_ROPE) ** -0.5
    attn = functools.partial(_dense_attention, kv_heads=MLA_HEADS, groups=1, dk=MLA_QK_PAD, dv=MLA_V, scale=scale)
    kx, vx = _mla_keys_values(px, kv_norm, w_ukv, rope)
    kc, vc = _mla_keys_values(pc, kv_norm, w_ukv, None)
    qx = _mla_queries(px, q_norm, w_uq, rope)
    y_x = attn(qx, jnp.concatenate([kc, kx], axis=1), jnp.concatenate([vc, vx], axis=1))
    y_c = attn(_mla_queries(pc, q_norm, w_uq, None), kc, vc) if ctx_out else None
    return y_x, y_c


def _gqa_mixer(px, pc, q_norm, k_norm, rope, ctx_out):
    scale = HEAD_DIM ** -0.5
    attn = functools.partial(_dense_attention, kv_heads=GQA_KV_HEADS, groups=GQA_HEADS // GQA_KV_HEADS, dk=HEAD_DIM,
                             dv=HEAD_DIM, scale=scale)
    qx, kx, vx = _attn_heads(px, GQA_HEADS, GQA_KV_HEADS)
    qc, kc, vc = _attn_heads(pc, GQA_HEADS, GQA_KV_HEADS)
    qx = _apply_rope(_rms_norm(qx, q_norm), *rope)
    kx = _apply_rope(_rms_norm(kx, k_norm), *rope)
    kc = _rms_norm(kc, k_norm)
    k_all = _flat(jnp.concatenate([kc, kx], axis=1)).astype(BF16)
    v_all = _flat(jnp.concatenate([vc, vx], axis=1)).astype(BF16)
    y_x = attn(_flat(qx), k_all, v_all)
    y_c = attn(_flat(_rms_norm(qc, q_norm)), _flat(kc).astype(BF16), _flat(vc).astype(BF16)) if ctx_out else None
    return y_x, y_c


def _window_attention(q, k, v, kc, vc, sink, scale):
    bsz, s_len, hk, g, d = q.shape
    nb = s_len // WINDOW
    qb = q.reshape(bsz, nb, WINDOW, hk, g, d)
    pad = ((0, 0), (WINDOW, WINDOW), (0, 0), (0, 0))
    kp = jnp.pad(k, pad).reshape(bsz, nb + 2, WINDOW, hk, d)
    vp = jnp.pad(v, pad).reshape(bsz, nb + 2, WINDOW, hk, d)
    kband = jnp.concatenate([kp[:, :-2], kp[:, 1:-1], kp[:, 2:]], axis=2)
    vband = jnp.concatenate([vp[:, :-2], vp[:, 1:-1], vp[:, 2:]], axis=2)
    s_win = jnp.einsum('bnqhgd,bnkhd->bnhgqk', qb, kband).astype(F32) * scale
    blk = jnp.arange(nb)[:, None] * WINDOW
    q_pos = blk + jnp.arange(WINDOW)[None, :]
    k_pos = blk - WINDOW + jnp.arange(3 * WINDOW)[None, :]
    valid = (jnp.abs(k_pos[:, None, :] - q_pos[:, :, None]) <= WINDOW) & (k_pos >= 0)[:, None, :] & (k_pos < s_len)[:, None, :]
    s_win = jnp.where(valid[None, :, None, None], s_win, -jnp.inf)
    s_ctx = jnp.einsum('bnqhgd,bchd->bnhgqc', qb, kc).astype(F32) * scale
    s_sink = jnp.broadcast_to(sink.reshape(hk, g)[None, None, :, :, None, None].astype(F32), s_ctx.shape[:-1] + (1,))
    p = jax.nn.softmax(jnp.concatenate([s_ctx, s_win, s_sink], axis=-1), axis=-1).astype(v.dtype)
    n_ctx = kc.shape[1]
    o = jnp.einsum('bnhgqc,bchd->bnqhgd', p[..., :n_ctx], vc) + jnp.einsum('bnhgqk,bnkhd->bnqhgd', p[..., n_ctx:n_ctx + 3 * WINDOW], vband)
    return o.reshape(bsz, s_len, hk * g * d)


def _sink_attention(q, k, v, sink, scale):
    bsz, t_len, hk, g, d = q.shape
    s = jnp.einsum('bqhgd,bkhd->bhgqk', q, k).astype(F32) * scale
    s_sink = jnp.broadcast_to(sink.reshape(hk, g)[None, :, :, None, None].astype(F32), s.shape[:-1] + (1,))
    p = jax.nn.softmax(jnp.concatenate([s, s_sink], axis=-1), axis=-1)[..., :-1].astype(v.dtype)
    return jnp.einsum('bhgqk,bkhd->bqhgd', p, v).reshape(bsz, t_len, hk * g * d)


def _swa_mixer(px, pc, sink, rope, ctx_out):
    scale = HEAD_DIM ** -0.5
    qx, kx, vx = _attn_heads(px, SWA_HEADS, SWA_KV_HEADS)
    qc, kc, vc = _attn_heads(pc, SWA_HEADS, SWA_KV_HEADS)
    qx = _apply_rope(qx, *rope)
    kx = _apply_rope(kx, *rope)
    y_x = _window_attention(_group_q(qx, SWA_KV_HEADS), kx, vx, kc, vc, sink, scale)
    y_c = _sink_attention(_group_q(qc, SWA_KV_HEADS), kc, vc, sink, scale) if ctx_out else None
    return y_x, y_c


def _centred_dwconv(x, w, b):
    y = lax.conv_general_dilated(x, w[:, None, :], window_strides=(1,), padding=((SSD_CONV // 2, SSD_CONV // 2),),
                                 dimension_numbers=('NWC', 'WIO', 'NWC'), feature_group_count=x.shape[-1])
    return y + b


def _ssd_chunked(x, dt, A, Bm, Cm, h0, with_output):
    bsz, t_len = x.shape[:2]
    L, G, Hg, P, N = SSD_CHUNK, SSD_GROUPS, SSD_HEADS // SSD_GROUPS, SSD_HEAD_DIM, SSD_STATE
    nc = t_len // L
    xc = x.astype(F32).reshape(bsz, nc, L, G, Hg, P)
    dtc = dt.astype(F32).reshape(bsz, nc, L, G, Hg)
    Bc = Bm.astype(F32).reshape(bsz, nc, L, G, N)
    Cc = Cm.astype(F32).reshape(bsz, nc, L, G, N)
    a_cum = jnp.cumsum(dtc * A.astype(F32).reshape(G, Hg), axis=2)
    a_end = a_cum[:, :, -1]
    states = jnp.einsum('bclgn,bclgh,bclghp->bcghpn', Bc, jnp.exp(a_end[:, :, None] - a_cum) * dtc, xc)

    def carry_state(h, inp):
        decay, st = inp
        return h * decay[..., None, None] + st, h

    h_final, h_enter = lax.scan(carry_state, h0.astype(F32), (jnp.moveaxis(jnp.exp(a_end), 1, 0), jnp.moveaxis(states, 1, 0)))
    if not with_output:
        return None, h_final
    h_enter = jnp.moveaxis(h_enter, 0, 1)
    causal = jnp.tril(jnp.ones((L, L), dtype=bool))
    seg = a_cum[:, :, :, None] - a_cum[:, :, None]
    decay = jnp.exp(jnp.where(causal[:, :, None, None], seg, -jnp.inf))
    cb = jnp.einsum('bclgn,bcsgn->bclsg', Cc, Bc)
    y_diag = jnp.einsum('bclsgh,bcsghp->bclghp', cb[..., None] * decay * dtc[:, :, None], xc)
    y_off = jnp.einsum('bclgn,bcghpn->bclghp', Cc, h_enter) * jnp.exp(a_cum)[..., None]
    return (y_diag + y_off).reshape(bsz, t_len, SSD_HEADS, P), h_final


def _ssd_mixer(px, pc, conv_w, conv_b, a_log, dt_bias, d_skip, norm_g, ctx_out):
    def prep(p):
        bsz, t_len, _ = p.shape
        z, xbc, dt = jnp.split(p, [SSD_INNER, SSD_INNER + SSD_CONV_DIM], axis=-1)
        xbc = jax.nn.silu(_centred_dwconv(xbc, conv_w, conv_b))
        xs, Bm, Cm = jnp.split(xbc, [SSD_INNER, SSD_INNER + SSD_GROUPS * SSD_STATE], axis=-1)
        xs = xs.reshape(bsz, t_len, SSD_HEADS, SSD_HEAD_DIM)
        Bm = Bm.reshape(bsz, t_len, SSD_GROUPS, SSD_STATE)
        Cm = Cm.reshape(bsz, t_len, SSD_GROUPS, SSD_STATE)
        dt = jax.nn.softplus(dt.reshape(bsz, t_len, 2, SSD_HEADS) + dt_bias)
        return z, xs, Bm, Cm, dt

    def flip(a):
        return jnp.flip(a, axis=1)

    A = -jnp.exp(a_log.astype(F32))
    zx, xx, Bx, Cx, dtx = prep(px)
    zc, xc, Bc, Cc, dtc = prep(pc)
    h0 = jnp.zeros((px.shape[0], SSD_GROUPS, SSD_HEADS // SSD_GROUPS, SSD_HEAD_DIM, SSD_STATE), F32)
    yc_f, hc_f = _ssd_chunked(xc, dtc[:, :, 0], A[0], Bc, Cc, h0, ctx_out)
    yx_f, _ = _ssd_chunked(xx, dtx[:, :, 0], A[0], Bx, Cx, hc_f, True)
    yc_b, hc_b = _ssd_chunked(flip(xc), flip(dtc[:, :, 1]), A[1], flip(Bc), flip(Cc), h0, ctx_out)
    yx_b, _ = _ssd_chunked(flip(xx), flip(dtx[:, :, 1]), A[1], flip(Bx), flip(Cx), hc_b, True)

    def finish(y_f, y_b, xs, z):
        y = (y_f + flip(y_b) + d_skip.astype(F32)[:, None] * xs.astype(F32)).astype(xs.dtype)
        bsz, t_len = xs.shape[:2]
        return _rms_norm(y.reshape(bsz, t_len, SSD_INNER) * jax.nn.silu(z), norm_g)

    y_x = finish(yx_f, yx_b, xx, zx)
    y_c = finish(yc_f, yc_b, xc, zc) if ctx_out else None
    return y_x, y_c


PEER_SEL = PEER_HEADS * PEER_TOPK
PEER_HALF = PEER_DKEY // 2
PEER_ROUTE_TILE = 128
PEER_APPLY_TILE = 128
PEER_GROUP = 8
PEER_SLOTS = 8
PEER_LOOKAHEAD = 6
PEER_CHUNK = 256


def _top_extract(vals, iota, count, payload=None):
    n_rows = vals.shape[0]
    top_v, top_i = [], []
    for _ in range(count):
        m = jnp.max(vals, axis=0, keepdims=True)
        idx = jnp.min(jnp.where(vals == m, iota, n_rows), axis=0, keepdims=True)
        sel = iota == idx
        top_v.append(m)
        top_i.append(idx if payload is None else jnp.sum(jnp.where(sel, payload, 0), axis=0, keepdims=True))
        vals = jnp.where(sel, -jnp.inf, vals)
    return jnp.concatenate(top_v, axis=0), jnp.concatenate(top_i, axis=0)


def _peer_route_kernel(q_ref, keys_ref, e_ref, g_ref, et_ref, gt_ref):
    t_len = q_ref.shape[0]
    iota_k = lax.broadcasted_iota(jnp.int32, (PEER_KEYS, t_len), 0)
    iota_c = lax.broadcasted_iota(jnp.int32, (PEER_TOPK + (PEER_TOPK // 2) ** 2, t_len), 0)

    def head(h, carry):
        tops = []
        for p in range(2):
            col = pl.multiple_of((2 * h + p) * PEER_HALF, PEER_HALF)
            qs = q_ref[:, pl.ds(col, PEER_HALF)].astype(BF16)
            s_t = lax.dot_general(keys_ref[2 * h + p], qs, (((1,), (1,)), ((), ())), preferred_element_type=F32)
            tops.append(_top_extract(s_t, iota_k, PEER_TOPK))
        (s1, i1), (s2, i2) = tops
        half = PEER_TOPK // 2
        cand = jnp.concatenate([s1[0:1] + s2] + [s1[a:a + 1] + s2[:half] for a in range(1, half)]
                               + [s1[half:] + s2[0:1]], axis=0)
        i1s = i1 * PEER_KEYS
        expert = jnp.concatenate([i1s[0:1] + i2] + [i1s[a:a + 1] + i2[:half] for a in range(1, half)]
                                 + [i1s[half:] + i2[0:1]], axis=0)
        sc, e_h = _top_extract(cand, iota_c, PEER_TOPK, payload=expert)
        p_h = jnp.exp(sc - sc[0:1])
        row = pl.multiple_of(h * PEER_TOPK, PEER_TOPK)
        gt_ref[pl.ds(row, PEER_TOPK), :] = p_h / jnp.sum(p_h, axis=0, keepdims=True)
        et_ref[pl.ds(row, PEER_TOPK), :] = e_h
        return carry

    lax.fori_loop(0, PEER_HEADS, head, 0)
    e_ref[...] = et_ref[...].T
    g_ref[...] = gt_ref[...].T


def _peer_route(q, sub_keys):
    t_len = q.shape[0]
    tile = _largest_tile(t_len, PEER_ROUTE_TILE, LANES)
    keys = sub_keys.reshape(PEER_HEADS * 2, PEER_KEYS, PEER_HALF).astype(BF16)
    return pl.pallas_call(
        _peer_route_kernel,
        name="peer_route",
        grid=(t_len // tile,),
        in_specs=[pl.BlockSpec((tile, PEER_HEADS * PEER_DKEY), lambda i: (i, 0)),
                  pl.BlockSpec((PEER_HEADS * 2, PEER_KEYS, PEER_HALF), lambda i: (0, 0, 0))],
        out_specs=[pl.BlockSpec((tile, PEER_SEL), lambda i: (i, 0)),
                   pl.BlockSpec((tile, PEER_SEL), lambda i: (i, 0))],
        out_shape=[jax.ShapeDtypeStruct((t_len, PEER_SEL), jnp.int32), jax.ShapeDtypeStruct((t_len, PEER_SEL), F32)],
        scratch_shapes=[pltpu.VMEM((PEER_SEL, tile), jnp.int32), pltpu.VMEM((PEER_SEL, tile), F32)],
        compiler_params=pltpu.CompilerParams(dimension_semantics=("arbitrary",), vmem_limit_bytes=VMEM_LIMIT_BYTES),
    )(q, keys)


def _pack_peer_table(u, v):
    ub = lax.bitcast_convert_type(u.astype(jnp.bfloat16), jnp.uint16).astype(jnp.uint32)
    vb = lax.bitcast_convert_type(v.astype(jnp.bfloat16), jnp.uint16).astype(jnp.uint32)
    return (ub << 16) | vb


def _peer_apply_kernel(e_ref, h_ref, sh_ref, sc_ref, g_ref, gate_ref, lng_ref, lnb_ref, tab_ref, rows_ref, o_ref, buf,
                       sem, *, alpha):
    t_len = h_ref.shape[1]
    n_groups = t_len // PEER_GROUP

    def row_copy(tok, j, slot):
        return pltpu.make_async_copy(rows_ref.at[e_ref[tok, j]], buf.at[slot, pl.ds(j, 1)], sem.at[slot])

    d = h_ref.shape[2]
    chunk = min(d, PEER_CHUNK)
    n_chunks = d // chunk
    per_chunk = PEER_SEL // (2 * n_chunks)

    def request(tok, slot, rows):
        for j in rows:
            row_copy(tok, j, slot).start(priority=j % 2)

    for tok in range(PEER_LOOKAHEAD):
        request(tok, tok % PEER_SLOTS, range(PEER_SEL))

    row_id = lax.broadcasted_iota(jnp.int32, (PEER_GROUP, PEER_SEL), 0)
    shift, scale, gate = sh_ref[0], sc_ref[0], gate_ref[0]
    ln_g, ln_b = lng_ref[...], lnb_ref[...]

    def group(base, n_requests):
        h8 = h_ref[0, pl.ds(base, PEER_GROUP), :]
        hm8 = (h8 * (1.0 + scale) + shift).astype(BF16)
        g8 = g_ref[pl.ds(base, PEER_GROUP), :]

        ffn = [jnp.zeros((PEER_GROUP, chunk), F32) for _ in range(n_chunks)]
        for r in range(PEER_GROUP):
            pltpu.make_async_copy(tab_ref.at[pl.ds(0, PEER_SEL)], buf.at[r], sem.at[r]).wait()

            def interleaved_requests(step):
                if r < n_requests:
                    request(base + r + PEER_LOOKAHEAD, (r + PEER_LOOKAHEAD) % PEER_SLOTS,
                            range(step * per_chunk, (step + 1) * per_chunk))

            s = jnp.zeros((PEER_GROUP, PEER_SEL), F32)
            for c in range(n_chunks):
                interleaved_requests(c)
                words = buf[r, :, c * chunk:(c + 1) * chunk]
                u_rows = lax.bitcast_convert_type(words & jnp.uint32(0xFFFF0000), F32).astype(BF16)
                s = s + lax.dot_general(hm8[:, c * chunk:(c + 1) * chunk], u_rows, (((1,), (1,)), ((), ())),
                                        preferred_element_type=F32)
            w = jnp.where(row_id == r, g8 * jax.nn.gelu(s), 0.0).astype(BF16)
            for c in range(n_chunks):
                interleaved_requests(n_chunks + c)
                words = buf[r, :, c * chunk:(c + 1) * chunk]
                v_rows = lax.bitcast_convert_type(words << 16, F32).astype(BF16)
                ffn[c] = ffn[c] + jnp.dot(w, v_rows, preferred_element_type=F32)
        y = alpha * h8 + gate * jnp.concatenate(ffn, axis=1)
        mu = jnp.mean(y, axis=-1, keepdims=True)
        var = jnp.mean(jnp.square(y - mu), axis=-1, keepdims=True)
        o_ref[0, pl.ds(base, PEER_GROUP), :] = (y - mu) * lax.rsqrt(var + EPS) * ln_g + ln_b

    def full_group(gi, carry):
        group(pl.multiple_of(gi * PEER_GROUP, PEER_GROUP), PEER_GROUP)
        return carry

    lax.fori_loop(0, n_groups - 1, full_group, 0)
    group((n_groups - 1) * PEER_GROUP, PEER_GROUP - PEER_LOOKAHEAD)


def _peer_block(h, shift, scale, gate, w_q, sub_keys, table, ln_g, ln_b, alpha):
    bsz, t_len, d = h.shape
    q = _linear(h, w_q, shift, scale)
    e, g = _peer_route(q.reshape(bsz * t_len, -1), sub_keys)
    tile = _largest_tile(t_len, PEER_APPLY_TILE, PEER_GROUP)
    n_t = t_len // tile
    row_spec = pl.BlockSpec((1, 1, d), lambda b, i: (b, 0, 0))
    vec_spec = pl.BlockSpec((1, d), lambda b, i: (0, 0))
    return pl.pallas_call(
        functools.partial(_peer_apply_kernel, alpha=alpha),
        name="peer_apply",
        grid=(bsz, n_t),
        in_specs=[
            pl.BlockSpec((tile, PEER_SEL), lambda b, i: (b * n_t + i, 0), memory_space=pltpu.SMEM),
            pl.BlockSpec((1, tile, d), lambda b, i: (b, i, 0)),
            row_spec, row_spec,
            pl.BlockSpec((tile, PEER_SEL), lambda b, i: (b * n_t + i, 0)),
            row_spec, vec_spec, vec_spec,
            pl.BlockSpec(memory_space=pl.ANY),
            pl.BlockSpec(memory_space=pl.ANY),
        ],
        out_specs=pl.BlockSpec((1, tile, d), lambda b, i: (b, i, 0)),
        out_shape=jax.ShapeDtypeStruct((bsz, t_len, d), F32),
        scratch_shapes=[pltpu.VMEM((PEER_SLOTS, PEER_SEL, d), jnp.uint32), pltpu.SemaphoreType.DMA((PEER_SLOTS,))],
        compiler_params=pltpu.CompilerParams(dimension_semantics=("arbitrary", "arbitrary"),
                                             vmem_limit_bytes=VMEM_LIMIT_BYTES),
    )(e, h, shift, scale, g, gate, ln_g.reshape(1, d), ln_b.reshape(1, d), table, table.reshape(-1, 1, d))


def kernel(x, c, ctx, c_ctx, w_ada, b_ada, w_in, mla_q_norm, mla_w_uq, mla_kv_norm, mla_w_ukv, gqa_q_norm, gqa_k_norm, ssd_conv_w, ssd_conv_b, ssd_a_log, ssd_dt_bias, ssd_d, ssd_norm, swa_sink, w_out, ln1_g, ln1_b, peer_w_q, peer_sub_keys, peer_u, peer_v, ln2_g, ln2_b):
    bsz = x.shape[0]
    depth = w_in.shape[0]
    rows = x.shape[1] // GRID_W
    rope_mla = _axial_rope_tables(rows, MLA_ROPE)
    rope_head = _axial_rope_tables(rows, HEAD_DIM)
    alpha = (2.0 * depth) ** 0.25
    silu_c = jax.nn.silu(c)[:, None, :]
    silu_cc = jax.nn.silu(c_ctx)[None, None, :]
    for l in range(depth):
        ctx_out = l < depth - 1
        mx = jnp.split(silu_c @ w_ada[l] + b_ada[l], 6, axis=-1)
        mc = [jnp.broadcast_to(m, (bsz,) + m.shape[1:]) for m in jnp.split(silu_cc @ w_ada[l] + b_ada[l], 6, axis=-1)]
        w_in_l = w_in[l].astype(BF16)
        px = jnp.split(_linear(x, w_in_l, mx[0], mx[1]), GROUP_SPLITS, axis=-1)
        pc = jnp.split(_linear(ctx, w_in_l, mc[0], mc[1]), GROUP_SPLITS, axis=-1)
        w_uq = mla_w_uq[l].astype(BF16)
        w_ukv = mla_w_ukv[l].astype(BF16)
        ya_x, ya_c = _mla_mixer(px[0], pc[0], mla_q_norm[l], w_uq, mla_kv_norm[l], w_ukv, rope_mla, ctx_out)
        yb_x, yb_c = _gqa_mixer(px[1], pc[1], gqa_q_norm[l], gqa_k_norm[l], rope_head, ctx_out)
        yc_x, yc_c = _ssd_mixer(px[2], pc[2], ssd_conv_w[l], ssd_conv_b[l], ssd_a_log[l], ssd_dt_bias[l], ssd_d[l], ssd_norm[l], ctx_out)
        yd_x, yd_c = _swa_mixer(px[3], pc[3], swa_sink[l], rope_head, ctx_out)
        w_out_l = w_out[l].astype(BF16)
        w_pq = peer_w_q[l].astype(BF16)
        table = _pack_peer_table(peer_u[l], peer_v[l])
        x = _linear_residual_ln(jnp.concatenate([ya_x, yb_x, yc_x, yd_x], axis=-1), w_out_l, x, mx[2], ln1_g[l],
                                ln1_b[l], alpha)
        x = _peer_block(x, mx[3], mx[4], mx[5], w_pq, peer_sub_keys[l], table, ln2_g[l], ln2_b[l], alpha)
        if ctx_out:
            ctx = _linear_residual_ln(jnp.concatenate([ya_c, yb_c, yc_c, yd_c], axis=-1), w_out_l, ctx, mc[2],
                                      ln1_g[l], ln1_b[l], alpha)
            ctx = _peer_block(ctx, mc[3], mc[4], mc[5], w_pq, peer_sub_keys[l], table, ln2_g[l], ln2_b[l], alpha)
    return x
```

```python
import functools
import math

import jax
import jax.numpy as jnp
from jax import lax
from jax.experimental import pallas as pl
from jax.experimental.pallas import tpu as pltpu

GRID_W = 64
ROPE_BASE = 10000.0
EPS = 1e-6
HEAD_DIM = 128

MLA_HEADS = 4
MLA_Q_RANK = 512
MLA_KV_RANK = 256
MLA_NOPE = 128
MLA_ROPE = 64
MLA_V = 128
MLA_QK_PAD = 256

GQA_HEADS = 4
GQA_KV_HEADS = 2

SSD_HEADS = 8
SSD_HEAD_DIM = 64
SSD_INNER = SSD_HEADS * SSD_HEAD_DIM
SSD_GROUPS = 2
SSD_STATE = 128
SSD_CONV = 5
SSD_CHUNK = 128
SSD_CONV_DIM = SSD_INNER + 2 * SSD_GROUPS * SSD_STATE

SWA_HEADS = 4
SWA_KV_HEADS = 2
WINDOW = 128

PEER_HEADS = 8
PEER_KEYS = 128
PEER_TOPK = 16
PEER_DKEY = 256
PEER_BLOCK = 128

MLA_COLS = MLA_Q_RANK + MLA_KV_RANK + MLA_ROPE
GQA_COLS = (GQA_HEADS + 2 * GQA_KV_HEADS) * HEAD_DIM
SSD_COLS = SSD_INNER + SSD_CONV_DIM + 2 * SSD_HEADS
SWA_COLS = (SWA_HEADS + 2 * SWA_KV_HEADS) * HEAD_DIM
GROUP_SPLITS = [MLA_COLS, MLA_COLS + GQA_COLS, MLA_COLS + GQA_COLS + SSD_COLS]

VMEM_LIMIT_BYTES = 56 * 1024 * 1024
LANES = 128

F32 = jnp.float32
BF16 = jnp.bfloat16


def _largest_tile(n, cap, unit):
    return max(t for t in range(unit, min(n, cap) + 1, unit) if n % t == 0)


def _linear_mod_kernel(x_ref, sh_ref, sc_ref, w_ref, o_ref):
    xm = x_ref[0] * (1.0 + sc_ref[0]) + sh_ref[0]
    o_ref[0] = jnp.dot(xm.astype(BF16), w_ref[...], preferred_element_type=F32)


def _linear_kernel(x_ref, w_ref, o_ref):
    o_ref[0] = jnp.dot(x_ref[0].astype(BF16), w_ref[...], preferred_element_type=F32)


def _linear(x, w, shift=None, scale=None, tm=512):
    bsz, t_len, k_dim = x.shape
    n_dim = w.shape[1]
    n_pad = -(-n_dim // LANES) * LANES
    if n_pad != n_dim:
        w = jnp.pad(w, ((0, 0), (0, n_pad - n_dim)))
    tn = _largest_tile(n_pad, 1024, LANES)
    tm = _largest_tile(t_len, tm, 8)
    grid = (bsz, t_len // tm, n_pad // tn)
    x_spec = pl.BlockSpec((1, tm, k_dim), lambda b, i, j: (b, i, 0))
    w_spec = pl.BlockSpec((k_dim, tn), lambda b, i, j: (0, j))
    o_spec = pl.BlockSpec((1, tm, tn), lambda b, i, j: (b, i, j))
    params = pltpu.CompilerParams(dimension_semantics=("arbitrary", "arbitrary", "arbitrary"),
                                  vmem_limit_bytes=VMEM_LIMIT_BYTES)
    out_shape = jax.ShapeDtypeStruct((bsz, t_len, n_pad), F32)
    if shift is None:
        out = pl.pallas_call(_linear_kernel, grid=grid, in_specs=[x_spec, w_spec], out_specs=o_spec,
                             out_shape=out_shape, compiler_params=params, name="linear")(x, w)
    else:
        m_spec = pl.BlockSpec((1, 1, k_dim), lambda b, i, j: (b, 0, 0))
        out = pl.pallas_call(_linear_mod_kernel, grid=grid, in_specs=[x_spec, m_spec, m_spec, w_spec], out_specs=o_spec,
                             out_shape=out_shape, compiler_params=params, name="linear_mod")(x, shift, scale, w)
    return out if n_pad == n_dim else out[..., :n_dim]


def _linear_residual_ln_kernel(a_ref, w_ref, x_ref, gate_ref, g_ref, b_ref, o_ref, *, alpha):
    y = alpha * x_ref[0] + gate_ref[0] * jnp.dot(a_ref[0].astype(BF16), w_ref[...], preferred_element_type=F32)
    mu = jnp.mean(y, axis=-1, keepdims=True)
    var = jnp.mean(jnp.square(y - mu), axis=-1, keepdims=True)
    o_ref[0] = (y - mu) * lax.rsqrt(var + EPS) * g_ref[...] + b_ref[...]


def _linear_residual_ln(a, w, x, gate, ln_g, ln_b, alpha, tm=256):
    bsz, t_len, k_dim = a.shape
    d = w.shape[1]
    tm = _largest_tile(t_len, tm, 8)
    row = pl.BlockSpec((1, tm, d), lambda b, i: (b, i, 0))
    vec = pl.BlockSpec((1, d), lambda b, i: (0, 0))
    return pl.pallas_call(
        functools.partial(_linear_residual_ln_kernel, alpha=alpha),
        name="linear_residual_ln",
        grid=(bsz, t_len // tm),
        in_specs=[pl.BlockSpec((1, tm, k_dim), lambda b, i: (b, i, 0)), pl.BlockSpec((k_dim, d), lambda b, i: (0, 0)),
                  row, pl.BlockSpec((1, 1, d), lambda b, i: (b, 0, 0)), vec, vec],
        out_specs=row,
        out_shape=jax.ShapeDtypeStruct((bsz, t_len, d), F32),
        compiler_params=pltpu.CompilerParams(dimension_semantics=("arbitrary", "arbitrary"),
                                             vmem_limit_bytes=VMEM_LIMIT_BYTES),
    )(a, w, x, gate, ln_g.reshape(1, d), ln_b.reshape(1, d))


def _dense_attn_kernel(q_ref, k_ref, v_ref, o_ref, s_even, s_odd, m_ref, acc_ref, *, groups, dk, dv, tk, scale):
    tq = q_ref.shape[1]
    n_chunks = k_ref.shape[1] // tk
    q = q_ref[0] * scale
    q = jnp.concatenate([q[:, g * dk:(g + 1) * dk] for g in range(groups)], axis=0).astype(BF16)
    m_ref[...] = jnp.full(m_ref.shape, -jnp.inf, F32)
    acc_ref[...] = jnp.zeros(acc_ref.shape, F32)
    ones_column = (lax.broadcasted_iota(jnp.int32, (tk, LANES), 1) == 0).astype(BF16)

    def scores(j, s_ref):
        kb = k_ref[0, pl.ds(pl.multiple_of(j * tk, tk), tk), :]
        s_ref[...] = lax.dot_general(q, kb, (((1,), (1,)), ((), ())), preferred_element_type=F32)

    def accumulate(j, s_ref):
        vb = jnp.concatenate([v_ref[0, pl.ds(pl.multiple_of(j * tk, tk), tk), :], ones_column], axis=1)
        s = s_ref[...]
        m_old = m_ref[...]
        m_new = jnp.maximum(m_old, jnp.max(s, axis=-1, keepdims=True))
        p = jnp.exp(s - m_new).astype(BF16)
        acc_ref[...] = jnp.exp(m_old - m_new) * acc_ref[...] + jnp.dot(p, vb, preferred_element_type=F32)
        m_ref[...] = m_new

    n_pairs = (n_chunks - 1) // 2
    scores(0, s_even)

    def pair(i, carry):
        scores(2 * i + 1, s_odd)
        accumulate(2 * i, s_even)
        scores(2 * i + 2, s_even)
        accumulate(2 * i + 1, s_odd)
        return carry

    lax.fori_loop(0, n_pairs, pair, 0)
    if n_chunks % 2 == 0:
        scores(n_chunks - 1, s_odd)
        accumulate(n_chunks - 2, s_even)
        accumulate(n_chunks - 1, s_odd)
    else:
        accumulate(n_chunks - 1, s_even)
    acc = acc_ref[...]
    o = acc[:, :dv] / acc[:, dv:dv + 1]
    o_ref[0] = jnp.concatenate([o[g * tq:(g + 1) * tq] for g in range(groups)], axis=1)


def _dense_attention(q, k, v, *, kv_heads, groups, dk, dv, scale, rows=512, tk=1280):
    bsz, s_len, _ = q.shape
    sk = k.shape[1]
    tq = _largest_tile(s_len, rows // groups, 8)
    tk = _largest_tile(sk, tk, LANES)
    kern = functools.partial(_dense_attn_kernel, groups=groups, dk=dk, dv=dv, tk=tk, scale=scale)
    return pl.pallas_call(
        kern,
        name="dense_attn",
        grid=(bsz, kv_heads, s_len // tq),
        in_specs=[
            pl.BlockSpec((1, tq, groups * dk), lambda b, h, i: (b, i, h)),
            pl.BlockSpec((1, sk, dk), lambda b, h, i: (b, 0, h)),
            pl.BlockSpec((1, sk, dv), lambda b, h, i: (b, 0, h)),
        ],
        out_specs=pl.BlockSpec((1, tq, groups * dv), lambda b, h, i: (b, i, h)),
        out_shape=jax.ShapeDtypeStruct((bsz, s_len, kv_heads * groups * dv), F32),
        scratch_shapes=[
            pltpu.VMEM((groups * tq, tk), F32),
            pltpu.VMEM((groups * tq, tk), F32),
            pltpu.VMEM((groups * tq, 1), F32),
            pltpu.VMEM((groups * tq, dv + LANES), F32),
        ],
        compiler_params=pltpu.CompilerParams(dimension_semantics=("arbitrary", "arbitrary", "arbitrary"),
                                             vmem_limit_bytes=VMEM_LIMIT_BYTES),
    )(q, k, v)


def _rms_norm(x, g):
    xf = x.astype(F32)
    y = xf * lax.rsqrt(jnp.mean(xf * xf, axis=-1, keepdims=True) + EPS)
    return (y * g.astype(F32)).astype(x.dtype)


def _layer_norm(x, g, b):
    xf = x.astype(F32)
    mu = jnp.mean(xf, axis=-1, keepdims=True)
    var = jnp.mean(jnp.square(xf - mu), axis=-1, keepdims=True)
    y = (xf - mu) * lax.rsqrt(var + EPS) * g.astype(F32) + b.astype(F32)
    return y.astype(x.dtype)


def _modulate(x, shift, scale):
    return x * (1 + scale) + shift


def _axial_rope_tables(rows, rot_dim):
    row = jnp.repeat(jnp.arange(rows, dtype=F32), GRID_W)
    col = jnp.tile(jnp.arange(GRID_W, dtype=F32), rows)
    n_freq = rot_dim // 4
    inv_freq = ROPE_BASE ** (-jnp.arange(n_freq, dtype=F32) / n_freq)
    ang = jnp.concatenate([row[:, None] * inv_freq, col[:, None] * inv_freq], axis=-1)
    return jnp.cos(ang), jnp.sin(ang)


def _apply_rope(x, cos, sin):
    half = x.shape[-1] // 2
    x1, x2 = x[..., :half], x[..., half:]
    c = cos[None, :, None, :].astype(x.dtype)
    s = sin[None, :, None, :].astype(x.dtype)
    return jnp.concatenate([x1 * c - x2 * s, x1 * s + x2 * c], axis=-1)


def _attn_heads(p, n_q, n_kv):
    bsz, t_len, _ = p.shape
    q, k, v = jnp.split(p, [n_q * HEAD_DIM, (n_q + n_kv) * HEAD_DIM], axis=-1)
    return (q.reshape(bsz, t_len, n_q, HEAD_DIM), k.reshape(bsz, t_len, n_kv, HEAD_DIM), v.reshape(bsz, t_len, n_kv, HEAD_DIM))


def _group_q(q, n_kv):
    bsz, t_len, h, d = q.shape
    return q.reshape(bsz, t_len, n_kv, h // n_kv, d)


def _flat(a):
    return a.reshape(a.shape[0], a.shape[1], -1)


def _mla_queries(p, q_norm, w_uq, rope):
    bsz, t_len, _ = p.shape
    q = _linear(_rms_norm(p[..., :MLA_Q_RANK], q_norm), w_uq).reshape(bsz, t_len, MLA_HEADS, MLA_NOPE + MLA_ROPE)
    if rope is not None:
        q = jnp.concatenate([q[..., :MLA_NOPE], _apply_rope(q[..., MLA_NOPE:], *rope)], axis=-1)
    pad = jnp.zeros((bsz, t_len, MLA_HEADS, MLA_QK_PAD - MLA_NOPE - MLA_ROPE), q.dtype)
    return _flat(jnp.concatenate([q, pad], axis=-1))


def _mla_keys_values(p, kv_norm, w_ukv, rope):
    bsz, t_len, _ = p.shape
    ckv = p[..., MLA_Q_RANK:MLA_Q_RANK + MLA_KV_RANK]
    k_rope = p[..., MLA_Q_RANK + MLA_KV_RANK:][:, :, None, :]
    if rope is not None:
        k_rope = _apply_rope(k_rope, *rope)
    kv = _linear(_rms_norm(ckv, kv_norm), w_ukv).reshape(bsz, t_len, MLA_HEADS, MLA_NOPE + MLA_V)
    pad = jnp.zeros((bsz, t_len, MLA_HEADS, MLA_QK_PAD - MLA_NOPE - MLA_ROPE), kv.dtype)
    k = jnp.concatenate([kv[..., :MLA_NOPE], jnp.broadcast_to(k_rope, (bsz, t_len, MLA_HEADS, MLA_ROPE)), pad], axis=-1)
    return _flat(k).astype(BF16), _flat(kv[..., MLA_NOPE:]).astype(BF16)


def _mla_mixer(px, pc, q_norm, w_uq, kv_norm, w_ukv, rope, ctx_out):
    scale = (MLA_NOPE + MLA_ROPE) ** -0.5
    attn = functools.partial(_dense_attention, kv_heads=MLA_HEADS, groups=1, dk=MLA_QK_PAD, dv=MLA_V, scale=scale)
    kx, vx = _mla_keys_values(px, kv_norm, w_ukv, rope)
    kc, vc = _mla_keys_values(pc, kv_norm, w_ukv, None)
    qx = _mla_queries(px, q_norm, w_uq, rope)
    y_x = attn(qx, jnp.concatenate([kc, kx], axis=1), jnp.concatenate([vc, vx], axis=1))
    y_c = attn(_mla_queries(pc, q_norm, w_uq, None), kc, vc) if ctx_out else None
    return y_x, y_c


def _gqa_mixer(px, pc, q_norm, k_norm, rope, ctx_out):
    scale = HEAD_DIM ** -0.5
    attn = functools.partial(_dense_attention, kv_heads=GQA_KV_HEADS, groups=GQA_HEADS // GQA_KV_HEADS, dk=HEAD_DIM,
                             dv=HEAD_DIM, scale=scale)
    qx, kx, vx = _attn_heads(px, GQA_HEADS, GQA_KV_HEADS)
    qc, kc, vc = _attn_heads(pc, GQA_HEADS, GQA_KV_HEADS)
    qx = _apply_rope(_rms_norm(qx, q_norm), *rope)
    kx = _apply_rope(_rms_norm(kx, k_norm), *rope)
    kc = _rms_norm(kc, k_norm)
    k_all = _flat(jnp.concatenate([kc, kx], axis=1)).astype(BF16)
    v_all = _flat(jnp.concatenate([vc, vx], axis=1)).astype(BF16)
    y_x = attn(_flat(qx), k_all, v_all)
    y_c = attn(_flat(_rms_norm(qc, q_norm)), _flat(kc).astype(BF16), _flat(vc).astype(BF16)) if ctx_out else None
    return y_x, y_c


def _window_attention(q, k, v, kc, vc, sink, scale):
    bsz, s_len, hk, g, d = q.shape
    nb = s_len // WINDOW
    qb = q.reshape(bsz, nb, WINDOW, hk, g, d)
    pad = ((0, 0), (WINDOW, WINDOW), (0, 0), (0, 0))
    kp = jnp.pad(k, pad).reshape(bsz, nb + 2, WINDOW, hk, d)
    vp = jnp.pad(v, pad).reshape(bsz, nb + 2, WINDOW, hk, d)
    kband = jnp.concatenate([kp[:, :-2], kp[:, 1:-1], kp[:, 2:]], axis=2)
    vband = jnp.concatenate([vp[:, :-2], vp[:, 1:-1], vp[:, 2:]], axis=2)
    s_win = jnp.einsum('bnqhgd,bnkhd->bnhgqk', qb, kband).astype(F32) * scale
    blk = jnp.arange(nb)[:, None] * WINDOW
    q_pos = blk + jnp.arange(WINDOW)[None, :]
    k_pos = blk - WINDOW + jnp.arange(3 * WINDOW)[None, :]
    valid = (jnp.abs(k_pos[:, None, :] - q_pos[:, :, None]) <= WINDOW) & (k_pos >= 0)[:, None, :] & (k_pos < s_len)[:, None, :]
    s_win = jnp.where(valid[None, :, None, None], s_win, -jnp.inf)
    s_ctx = jnp.einsum('bnqhgd,bchd->bnhgqc', qb, kc).astype(F32) * scale
    s_sink = jnp.broadcast_to(sink.reshape(hk, g)[None, None, :, :, None, None].astype(F32), s_ctx.shape[:-1] + (1,))
    p = jax.nn.softmax(jnp.concatenate([s_ctx, s_win, s_sink], axis=-1), axis=-1).astype(v.dtype)
    n_ctx = kc.shape[1]
    o = jnp.einsum('bnhgqc,bchd->bnqhgd', p[..., :n_ctx], vc) + jnp.einsum('bnhgqk,bnkhd->bnqhgd', p[..., n_ctx:n_ctx + 3 * WINDOW], vband)
    return o.reshape(bsz, s_len, hk * g * d)


def _sink_attention(q, k, v, sink, scale):
    bsz, t_len, hk, g, d = q.shape
    s = jnp.einsum('bqhgd,bkhd->bhgqk', q, k).astype(F32) * scale
    s_sink = jnp.broadcast_to(sink.reshape(hk, g)[None, :, :, None, None].astype(F32), s.shape[:-1] + (1,))
    p = jax.nn.softmax(jnp.concatenate([s, s_sink], axis=-1), axis=-1)[..., :-1].astype(v.dtype)
    return jnp.einsum('bhgqk,bkhd->bqhgd', p, v).reshape(bsz, t_len, hk * g * d)


def _swa_mixer(px, pc, sink, rope, ctx_out):
    scale = HEAD_DIM ** -0.5
    qx, kx, vx = _attn_heads(px, SWA_HEADS, SWA_KV_HEADS)
    qc, kc, vc = _attn_heads(pc, SWA_HEADS, SWA_KV_HEADS)
    qx = _apply_rope(qx, *rope)
    kx = _apply_rope(kx, *rope)
    y_x = _window_attention(_group_q(qx, SWA_KV_HEADS), kx, vx, kc, vc, sink, scale)
    y_c = _sink_attention(_group_q(qc, SWA_KV_HEADS), kc, vc, sink, scale) if ctx_out else None
    return y_x, y_c


def _centred_dwconv(x, w, b):
    y = lax.conv_general_dilated(x, w[:, None, :], window_strides=(1,), padding=((SSD_CONV // 2, SSD_CONV // 2),),
                                 dimension_numbers=('NWC', 'WIO', 'NWC'), feature_group_count=x.shape[-1])
    return y + b


def _ssd_chunked(x, dt, A, Bm, Cm, h0, with_output):
    bsz, t_len = x.shape[:2]
    L, G, Hg, P, N = SSD_CHUNK, SSD_GROUPS, SSD_HEADS // SSD_GROUPS, SSD_HEAD_DIM, SSD_STATE
    nc = t_len // L
    xc = x.astype(F32).reshape(bsz, nc, L, G, Hg, P)
    dtc = dt.astype(F32).reshape(bsz, nc, L, G, Hg)
    Bc = Bm.astype(F32).reshape(bsz, nc, L, G, N)
    Cc = Cm.astype(F32).reshape(bsz, nc, L, G, N)
    a_cum = jnp.cumsum(dtc * A.astype(F32).reshape(G, Hg), axis=2)
    a_end = a_cum[:, :, -1]
    states = jnp.einsum('bclgn,bclgh,bclghp->bcghpn', Bc, jnp.exp(a_end[:, :, None] - a_cum) * dtc, xc)

    def carry_state(h, inp):
        decay, st = inp
        return h * decay[..., None, None] + st, h

    h_final, h_enter = lax.scan(carry_state, h0.astype(F32), (jnp.moveaxis(jnp.exp(a_end), 1, 0), jnp.moveaxis(states, 1, 0)))
    if not with_output:
        return None, h_final
    h_enter = jnp.moveaxis(h_enter, 0, 1)
    causal = jnp.tril(jnp.ones((L, L), dtype=bool))
    seg = a_cum[:, :, :, None] - a_cum[:, :, None]
    decay = jnp.exp(jnp.where(causal[:, :, None, None], seg, -jnp.inf))
    cb = jnp.einsum('bclgn,bcsgn->bclsg', Cc, Bc)
    y_diag = jnp.einsum('bclsgh,bcsghp->bclghp', cb[..., None] * decay * dtc[:, :, None], xc)
    y_off = jnp.einsum('bclgn,bcghpn->bclghp', Cc, h_enter) * jnp.exp(a_cum)[..., None]
    return (y_diag + y_off).reshape(bsz, t_len, SSD_HEADS, P), h_final


def _ssd_mixer(px, pc, conv_w, conv_b, a_log, dt_bias, d_skip, norm_g, ctx_out):
    def prep(p):
        bsz, t_len, _ = p.shape
        z, xbc, dt = jnp.split(p, [SSD_INNER, SSD_INNER + SSD_CONV_DIM], axis=-1)
        xbc = jax.nn.silu(_centred_dwconv(xbc, conv_w, conv_b))
        xs, Bm, Cm = jnp.split(xbc, [SSD_INNER, SSD_INNER + SSD_GROUPS * SSD_STATE], axis=-1)
        xs = xs.reshape(bsz, t_len, SSD_HEADS, SSD_HEAD_DIM)
        Bm = Bm.reshape(bsz, t_len, SSD_GROUPS, SSD_STATE)
        Cm = Cm.reshape(bsz, t_len, SSD_GROUPS, SSD_STATE)
        dt = jax.nn.softplus(dt.reshape(bsz, t_len, 2, SSD_HEADS) + dt_bias)
        return z, xs, Bm, Cm, dt

    def flip(a):
        return jnp.flip(a, axis=1)

    A = -jnp.exp(a_log.astype(F32))
    zx, xx, Bx, Cx, dtx = prep(px)
    zc, xc, Bc, Cc, dtc = prep(pc)
    h0 = jnp.zeros((px.shape[0], SSD_GROUPS, SSD_HEADS // SSD_GROUPS, SSD_HEAD_DIM, SSD_STATE), F32)
    yc_f, hc_f = _ssd_chunked(xc, dtc[:, :, 0], A[0], Bc, Cc, h0, ctx_out)
    yx_f, _ = _ssd_chunked(xx, dtx[:, :, 0], A[0], Bx, Cx, hc_f, True)
    yc_b, hc_b = _ssd_chunked(flip(xc), flip(dtc[:, :, 1]), A[1], flip(Bc), flip(Cc), h0, ctx_out)
    yx_b, _ = _ssd_chunked(flip(xx), flip(dtx[:, :, 1]), A[1], flip(Bx), flip(Cx), hc_b, True)

    def finish(y_f, y_b, xs, z):
        y = (y_f + flip(y_b) + d_skip.astype(F32)[:, None] * xs.astype(F32)).astype(xs.dtype)
        bsz, t_len = xs.shape[:2]
        return _rms_norm(y.reshape(bsz, t_len, SSD_INNER) * jax.nn.silu(z), norm_g)

    y_x = finish(yx_f, yx_b, xx, zx)
    y_c = finish(yc_f, yc_b, xc, zc) if ctx_out else None
    return y_x, y_c


PEER_SEL = PEER_HEADS * PEER_TOPK
PEER_HALF = PEER_DKEY // 2
PEER_ROUTE_TILE = 128
PEER_APPLY_TILE = 128
PEER_GROUP = 8
PEER_SLOTS = 8
PEER_LOOKAHEAD = 6
PEER_CHUNK = 256


def _top_extract(vals, iota, count, payload=None):
    n_rows = vals.shape[0]
    top_v, top_i = [], []
    for _ in range(count):
        m = jnp.max(vals, axis=0, keepdims=True)
        idx = jnp.min(jnp.where(vals == m, iota, n_rows), axis=0, keepdims=True)
        sel = iota == idx
        top_v.append(m)
        top_i.append(idx if payload is None else jnp.sum(jnp.where(sel, payload, 0), axis=0, keepdims=True))
        vals = jnp.where(sel, -jnp.inf, vals)
    return jnp.concatenate(top_v, axis=0), jnp.concatenate(top_i, axis=0)


def _peer_route_kernel(q_ref, keys_ref, e_ref, g_ref, et_ref, gt_ref):
    t_len = q_ref.shape[0]
    iota_k = lax.broadcasted_iota(jnp.int32, (PEER_KEYS, t_len), 0)
    iota_c = lax.broadcasted_iota(jnp.int32, (PEER_TOPK + (PEER_TOPK // 2) ** 2, t_len), 0)

    def head(h, carry):
        tops = []
        for p in range(2):
            col = pl.multiple_of((2 * h + p) * PEER_HALF, PEER_HALF)
            qs = q_ref[:, pl.ds(col, PEER_HALF)].astype(BF16)
            s_t = lax.dot_general(keys_ref[2 * h + p], qs, (((1,), (1,)), ((), ())), preferred_element_type=F32)
            tops.append(_top_extract(s_t, iota_k, PEER_TOPK))
        (s1, i1), (s2, i2) = tops
        half = PEER_TOPK // 2
        cand = jnp.concatenate([s1[0:1] + s2] + [s1[a:a + 1] + s2[:half] for a in range(1, half)]
                               + [s1[half:] + s2[0:1]], axis=0)
        i1s = i1 * PEER_KEYS
        expert = jnp.concatenate([i1s[0:1] + i2] + [i1s[a:a + 1] + i2[:half] for a in range(1, half)]
                                 + [i1s[half:] + i2[0:1]], axis=0)
        sc, e_h = _top_extract(cand, iota_c, PEER_TOPK, payload=expert)
        p_h = jnp.exp(sc - sc[0:1])
        row = pl.multiple_of(h * PEER_TOPK, PEER_TOPK)
        gt_ref[pl.ds(row, PEER_TOPK), :] = p_h / jnp.sum(p_h, axis=0, keepdims=True)
        et_ref[pl.ds(row, PEER_TOPK), :] = e_h
        return carry

    lax.fori_loop(0, PEER_HEADS, head, 0)
    e_ref[...] = et_ref[...].T
    g_ref[...] = gt_ref[...].T


def _peer_route(q, sub_keys):
    t_len = q.shape[0]
    tile = _largest_tile(t_len, PEER_ROUTE_TILE, LANES)
    keys = sub_keys.reshape(PEER_HEADS * 2, PEER_KEYS, PEER_HALF).astype(BF16)
    return pl.pallas_call(
        _peer_route_kernel,
        name="peer_route",
        grid=(t_len // tile,),
        in_specs=[pl.BlockSpec((tile, PEER_HEADS * PEER_DKEY), lambda i: (i, 0)),
                  pl.BlockSpec((PEER_HEADS * 2, PEER_KEYS, PEER_HALF), lambda i: (0, 0, 0))],
        out_specs=[pl.BlockSpec((tile, PEER_SEL), lambda i: (i, 0)),
                   pl.BlockSpec((tile, PEER_SEL), lambda i: (i, 0))],
        out_shape=[jax.ShapeDtypeStruct((t_len, PEER_SEL), jnp.int32), jax.ShapeDtypeStruct((t_len, PEER_SEL), F32)],
        scratch_shapes=[pltpu.VMEM((PEER_SEL, tile), jnp.int32), pltpu.VMEM((PEER_SEL, tile), F32)],
        compiler_params=pltpu.CompilerParams(dimension_semantics=("arbitrary",), vmem_limit_bytes=VMEM_LIMIT_BYTES),
    )(q, keys)


def _pack_peer_table(u, v):
    ub = lax.bitcast_convert_type(u.astype(jnp.bfloat16), jnp.uint16).astype(jnp.uint32)
    vb = lax.bitcast_convert_type(v.astype(jnp.bfloat16), jnp.uint16).astype(jnp.uint32)
    return (ub << 16) | vb


def _peer_apply_kernel(e_ref, e_next_ref, h_ref, sh_ref, sc_ref, g_ref, gate_ref, lng_ref, lnb_ref, tab_ref, rows_ref,
                       o_ref, buf, sem, *, alpha):
    t_len = h_ref.shape[1]
    n_groups = t_len // PEER_GROUP
    step = pl.program_id(0)
    d = h_ref.shape[2]
    chunk = min(d, PEER_CHUNK)
    n_chunks = d // chunk
    per_chunk = PEER_SEL // (2 * n_chunks)

    def request(ids_ref, tok, slot, rows):
        for j in rows:
            pltpu.make_async_copy(rows_ref.at[ids_ref[tok, j]], buf.at[slot, pl.ds(j, 1)],
                                  sem.at[slot]).start(priority=j % 2)

    def arrived(slot):
        pltpu.make_async_copy(tab_ref.at[pl.ds(0, PEER_SEL)], buf.at[slot], sem.at[slot]).wait()

    @pl.when(step == 0)
    def _():
        for tok in range(PEER_LOOKAHEAD):
            request(e_ref, tok, tok % PEER_SLOTS, range(PEER_SEL))

    row_id = lax.broadcasted_iota(jnp.int32, (PEER_GROUP, PEER_SEL), 0)
    shift, scale, gate = sh_ref[0], sc_ref[0], gate_ref[0]
    ln_g, ln_b = lng_ref[...], lnb_ref[...]

    def group(base, last):
        h8 = h_ref[0, pl.ds(base, PEER_GROUP), :]
        hm8 = (h8 * (1.0 + scale) + shift).astype(BF16)
        g8 = g_ref[pl.ds(base, PEER_GROUP), :]
        ffn = [jnp.zeros((PEER_GROUP, chunk), F32) for _ in range(n_chunks)]
        for r in range(PEER_GROUP):
            arrived(r)
            ahead = r + PEER_LOOKAHEAD
            if last and ahead >= PEER_GROUP:
                ids_ref, tok = e_next_ref, ahead - PEER_GROUP
            else:
                ids_ref, tok = e_ref, base + ahead

            def interleaved_requests(part):
                request(ids_ref, tok, ahead % PEER_SLOTS, range(part * per_chunk, (part + 1) * per_chunk))

            s = jnp.zeros((PEER_GROUP, PEER_SEL), F32)
            for c in range(n_chunks):
                interleaved_requests(c)
                words = buf[r, :, c * chunk:(c + 1) * chunk]
                u_rows = lax.bitcast_convert_type(words & jnp.uint32(0xFFFF0000), F32).astype(BF16)
                s = s + lax.dot_general(hm8[:, c * chunk:(c + 1) * chunk], u_rows, (((1,), (1,)), ((), ())),
                                        preferred_element_type=F32)
            w = jnp.where(row_id == r, g8 * jax.nn.gelu(s), 0.0).astype(BF16)
            for c in range(n_chunks):
                interleaved_requests(n_chunks + c)
                words = buf[r, :, c * chunk:(c + 1) * chunk]
                v_rows = lax.bitcast_convert_type(words << 16, F32).astype(BF16)
                ffn[c] = ffn[c] + jnp.dot(w, v_rows, preferred_element_type=F32)
        y = alpha * h8 + gate * jnp.concatenate(ffn, axis=1)
        mu = jnp.mean(y, axis=-1, keepdims=True)
        var = jnp.mean(jnp.square(y - mu), axis=-1, keepdims=True)
        o_ref[0, pl.ds(base, PEER_GROUP), :] = (y - mu) * lax.rsqrt(var + EPS) * ln_g + ln_b

    def full_group(gi, carry):
        group(pl.multiple_of(gi * PEER_GROUP, PEER_GROUP), False)
        return carry

    lax.fori_loop(0, n_groups - 1, full_group, 0)
    group((n_groups - 1) * PEER_GROUP, True)

    @pl.when(step == pl.num_programs(0) - 1)
    def _():
        for tok in range(PEER_LOOKAHEAD):
            arrived(tok % PEER_SLOTS)


def _peer_block(h, shift, scale, gate, w_q, sub_keys, table, ln_g, ln_b, alpha):
    bsz, t_len, d = h.shape
    q = _linear(h, w_q, shift, scale)
    e, g = _peer_route(q.reshape(bsz * t_len, -1), sub_keys)
    tile = _largest_tile(t_len, PEER_APPLY_TILE, PEER_GROUP)
    n_t = t_len // tile
    n_steps = bsz * n_t
    groups_per_tile = tile // PEER_GROUP
    row_spec = pl.BlockSpec((1, 1, d), lambda i: (i // n_t, 0, 0))
    vec_spec = pl.BlockSpec((1, d), lambda i: (0, 0))
    tok_spec = pl.BlockSpec((1, tile, d), lambda i: (i // n_t, i % n_t, 0))
    return pl.pallas_call(
        functools.partial(_peer_apply_kernel, alpha=alpha),
        name="peer_apply",
        grid=(n_steps,),
        in_specs=[
            pl.BlockSpec((tile, PEER_SEL), lambda i: (i, 0), memory_space=pltpu.SMEM),
            pl.BlockSpec((PEER_GROUP, PEER_SEL), lambda i: (jnp.minimum(i + 1, n_steps - 1) * groups_per_tile, 0),
                         memory_space=pltpu.SMEM),
            tok_spec, row_spec, row_spec,
            pl.BlockSpec((tile, PEER_SEL), lambda i: (i, 0)),
            row_spec, vec_spec, vec_spec,
            pl.BlockSpec(memory_space=pl.ANY),
            pl.BlockSpec(memory_space=pl.ANY),
        ],
        out_specs=tok_spec,
        out_shape=jax.ShapeDtypeStruct((bsz, t_len, d), F32),
        scratch_shapes=[pltpu.VMEM((PEER_SLOTS, PEER_SEL, d), jnp.uint32), pltpu.SemaphoreType.DMA((PEER_SLOTS,))],
        compiler_params=pltpu.CompilerParams(dimension_semantics=("arbitrary",), vmem_limit_bytes=VMEM_LIMIT_BYTES),
    )(e, e, h, shift, scale, g, gate, ln_g.reshape(1, d), ln_b.reshape(1, d), table, table.reshape(-1, 1, d))


def kernel(x, c, ctx, c_ctx, w_ada, b_ada, w_in, mla_q_norm, mla_w_uq, mla_kv_norm, mla_w_ukv, gqa_q_norm, gqa_k_norm, ssd_conv_w, ssd_conv_b, ssd_a_log, ssd_dt_bias, ssd_d, ssd_norm, swa_sink, w_out, ln1_g, ln1_b, peer_w_q, peer_sub_keys, peer_u, peer_v, ln2_g, ln2_b):
    bsz = x.shape[0]
    depth = w_in.shape[0]
    rows = x.shape[1] // GRID_W
    rope_mla = _axial_rope_tables(rows, MLA_ROPE)
    rope_head = _axial_rope_tables(rows, HEAD_DIM)
    alpha = (2.0 * depth) ** 0.25
    silu_c = jax.nn.silu(c)[:, None, :]
    silu_cc = jax.nn.silu(c_ctx)[None, None, :]
    for l in range(depth):
        ctx_out = l < depth - 1
        mx = jnp.split(silu_c @ w_ada[l] + b_ada[l], 6, axis=-1)
        mc = [jnp.broadcast_to(m, (bsz,) + m.shape[1:]) for m in jnp.split(silu_cc @ w_ada[l] + b_ada[l], 6, axis=-1)]
        w_in_l = w_in[l].astype(BF16)
        px = jnp.split(_linear(x, w_in_l, mx[0], mx[1]), GROUP_SPLITS, axis=-1)
        pc = jnp.split(_linear(ctx, w_in_l, mc[0], mc[1]), GROUP_SPLITS, axis=-1)
        w_uq = mla_w_uq[l].astype(BF16)
        w_ukv = mla_w_ukv[l].astype(BF16)
        ya_x, ya_c = _mla_mixer(px[0], pc[0], mla_q_norm[l], w_uq, mla_kv_norm[l], w_ukv, rope_mla, ctx_out)
        yb_x, yb_c = _gqa_mixer(px[1], pc[1], gqa_q_norm[l], gqa_k_norm[l], rope_head, ctx_out)
        yc_x, yc_c = _ssd_mixer(px[2], pc[2], ssd_conv_w[l], ssd_conv_b[l], ssd_a_log[l], ssd_dt_bias[l], ssd_d[l], ssd_norm[l], ctx_out)
        yd_x, yd_c = _swa_mixer(px[3], pc[3], swa_sink[l], rope_head, ctx_out)
        w_out_l = w_out[l].astype(BF16)
        w_pq = peer_w_q[l].astype(BF16)
        table = _pack_peer_table(peer_u[l], peer_v[l])
        x = _linear_residual_ln(jnp.concatenate([ya_x, yb_x, yc_x, yd_x], axis=-1), w_out_l, x, mx[2], ln1_g[l],
                                ln1_b[l], alpha)
        x = _peer_block(x, mx[3], mx[4], mx[5], w_pq, peer_sub_keys[l], table, ln2_g[l], ln2_b[l], alpha)
        if ctx_out:
            ctx = _linear_residual_ln(jnp.concatenate([ya_c, yb_c, yc_c, yd_c], axis=-1), w_out_l, ctx, mc[2],
                                      ln1_g[l], ln1_b[l], alpha)
            ctx = _peer_block(ctx, mc[3], mc[4], mc[5], w_pq, peer_sub_keys[l], table, ln2_g[l], ln2_b[l], alpha)
    return x
```

```python
import functools
import math

import jax
import jax.numpy as jnp
from jax import lax
from jax.experimental import pallas as pl
from jax.experimental.pallas import tpu as pltpu

GRID_W = 64
ROPE_BASE = 10000.0
EPS = 1e-6
HEAD_DIM = 128

MLA_HEADS = 4
MLA_Q_RANK = 512
MLA_KV_RANK = 256
MLA_NOPE = 128
MLA_ROPE = 64
MLA_V = 128
MLA_QK_PAD = 256

GQA_HEADS = 4
GQA_KV_HEADS = 2

SSD_HEADS = 8
SSD_HEAD_DIM = 64
SSD_INNER = SSD_HEADS * SSD_HEAD_DIM
SSD_GROUPS = 2
SSD_STATE = 128
SSD_CONV = 5
SSD_CHUNK = 128
SSD_CONV_DIM = SSD_INNER + 2 * SSD_GROUPS * SSD_STATE

SWA_HEADS = 4
SWA_KV_HEADS = 2
WINDOW = 128

PEER_HEADS = 8
PEER_KEYS = 128
PEER_TOPK = 16
PEER_DKEY = 256
PEER_BLOCK = 128

MLA_COLS = MLA_Q_RANK + MLA_KV_RANK + MLA_ROPE
GQA_COLS = (GQA_HEADS + 2 * GQA_KV_HEADS) * HEAD_DIM
SSD_COLS = SSD_INNER + SSD_CONV_DIM + 2 * SSD_HEADS
SWA_COLS = (SWA_HEADS + 2 * SWA_KV_HEADS) * HEAD_DIM
GROUP_SPLITS = [MLA_COLS, MLA_COLS + GQA_COLS, MLA_COLS + GQA_COLS + SSD_COLS]

VMEM_LIMIT_BYTES = 56 * 1024 * 1024
LANES = 128

F32 = jnp.float32
BF16 = jnp.bfloat16


def _largest_tile(n, cap, unit):
    return max(t for t in range(unit, min(n, cap) + 1, unit) if n % t == 0)


def _linear_mod_kernel(x_ref, sh_ref, sc_ref, w_ref, o_ref):
    xm = x_ref[0] * (1.0 + sc_ref[0]) + sh_ref[0]
    o_ref[0] = jnp.dot(xm.astype(BF16), w_ref[...], preferred_element_type=F32)


def _linear_kernel(x_ref, w_ref, o_ref):
    o_ref[0] = jnp.dot(x_ref[0].astype(BF16), w_ref[...], preferred_element_type=F32)


def _linear(x, w, shift=None, scale=None, tm=512):
    bsz, t_len, k_dim = x.shape
    n_dim = w.shape[1]
    n_pad = -(-n_dim // LANES) * LANES
    if n_pad != n_dim:
        w = jnp.pad(w, ((0, 0), (0, n_pad - n_dim)))
    tn = _largest_tile(n_pad, 1024, LANES)
    tm = _largest_tile(t_len, tm, 8)
    grid = (bsz, t_len // tm, n_pad // tn)
    x_spec = pl.BlockSpec((1, tm, k_dim), lambda b, i, j: (b, i, 0))
    w_spec = pl.BlockSpec((k_dim, tn), lambda b, i, j: (0, j))
    o_spec = pl.BlockSpec((1, tm, tn), lambda b, i, j: (b, i, j))
    params = pltpu.CompilerParams(dimension_semantics=("arbitrary", "arbitrary", "arbitrary"),
                                  vmem_limit_bytes=VMEM_LIMIT_BYTES)
    out_shape = jax.ShapeDtypeStruct((bsz, t_len, n_pad), F32)
    if shift is None:
        out = pl.pallas_call(_linear_kernel, grid=grid, in_specs=[x_spec, w_spec], out_specs=o_spec,
                             out_shape=out_shape, compiler_params=params, name="linear")(x, w)
    else:
        m_spec = pl.BlockSpec((1, 1, k_dim), lambda b, i, j: (b, 0, 0))
        out = pl.pallas_call(_linear_mod_kernel, grid=grid, in_specs=[x_spec, m_spec, m_spec, w_spec], out_specs=o_spec,
                             out_shape=out_shape, compiler_params=params, name="linear_mod")(x, shift, scale, w)
    return out if n_pad == n_dim else out[..., :n_dim]


def _linear_residual_ln_kernel(*refs, alpha, widths):
    a_refs, (w_ref, x_ref, gate_ref, g_ref, b_ref, o_ref) = refs[:len(widths)], refs[len(widths):]
    mix, row = 0.0, 0
    for a_ref, width in zip(a_refs, widths):
        mix = mix + jnp.dot(a_ref[0].astype(BF16), w_ref[row:row + width, :], preferred_element_type=F32)
        row += width
    y = alpha * x_ref[0] + gate_ref[0] * mix
    mu = jnp.mean(y, axis=-1, keepdims=True)
    var = jnp.mean(jnp.square(y - mu), axis=-1, keepdims=True)
    o_ref[0] = (y - mu) * lax.rsqrt(var + EPS) * g_ref[...] + b_ref[...]


def _linear_residual_ln(pieces, w, x, gate, ln_g, ln_b, alpha, tm=256):
    bsz, t_len, d = x.shape
    widths = tuple(p.shape[2] for p in pieces)
    assert sum(widths) == w.shape[0]
    tm = _largest_tile(t_len, tm, 8)
    row = pl.BlockSpec((1, tm, d), lambda b, i: (b, i, 0))
    vec = pl.BlockSpec((1, d), lambda b, i: (0, 0))
    return pl.pallas_call(
        functools.partial(_linear_residual_ln_kernel, alpha=alpha, widths=widths),
        name="linear_residual_ln",
        grid=(bsz, t_len // tm),
        in_specs=[pl.BlockSpec((1, tm, width), lambda b, i: (b, i, 0)) for width in widths]
        + [pl.BlockSpec(w.shape, lambda b, i: (0, 0)), row, pl.BlockSpec((1, 1, d), lambda b, i: (b, 0, 0)), vec, vec],
        out_specs=row,
        out_shape=jax.ShapeDtypeStruct((bsz, t_len, d), F32),
        compiler_params=pltpu.CompilerParams(dimension_semantics=("arbitrary", "arbitrary"),
                                             vmem_limit_bytes=VMEM_LIMIT_BYTES),
    )(*pieces, w, x, gate, ln_g.reshape(1, d), ln_b.reshape(1, d))


def _dense_attn_kernel(q_ref, k_ref, v_ref, o_ref, s_even, s_odd, m_ref, acc_ref, *, groups, dk, dv, tk, scale):
    tq = q_ref.shape[1]
    n_chunks = k_ref.shape[1] // tk
    q = q_ref[0] * scale
    q = jnp.concatenate([q[:, g * dk:(g + 1) * dk] for g in range(groups)], axis=0).astype(BF16)
    m_ref[...] = jnp.full(m_ref.shape, -jnp.inf, F32)
    acc_ref[...] = jnp.zeros(acc_ref.shape, F32)
    ones_column = (lax.broadcasted_iota(jnp.int32, (tk, LANES), 1) == 0).astype(BF16)

    def scores(j, s_ref):
        kb = k_ref[0, pl.ds(pl.multiple_of(j * tk, tk), tk), :]
        s_ref[...] = lax.dot_general(q, kb, (((1,), (1,)), ((), ())), preferred_element_type=F32)

    def accumulate(j, s_ref):
        vb = jnp.concatenate([v_ref[0, pl.ds(pl.multiple_of(j * tk, tk), tk), :], ones_column], axis=1)
        s = s_ref[...]
        m_old = m_ref[...]
        m_new = jnp.maximum(m_old, jnp.max(s, axis=-1, keepdims=True))
        p = jnp.exp(s - m_new).astype(BF16)
        acc_ref[...] = jnp.exp(m_old - m_new) * acc_ref[...] + jnp.dot(p, vb, preferred_element_type=F32)
        m_ref[...] = m_new

    n_pairs = (n_chunks - 1) // 2
    scores(0, s_even)

    def pair(i, carry):
        scores(2 * i + 1, s_odd)
        accumulate(2 * i, s_even)
        scores(2 * i + 2, s_even)
        accumulate(2 * i + 1, s_odd)
        return carry

    lax.fori_loop(0, n_pairs, pair, 0)
    if n_chunks % 2 == 0:
        scores(n_chunks - 1, s_odd)
        accumulate(n_chunks - 2, s_even)
        accumulate(n_chunks - 1, s_odd)
    else:
        accumulate(n_chunks - 1, s_even)
    acc = acc_ref[...]
    o = acc[:, :dv] / acc[:, dv:dv + 1]
    o_ref[0] = jnp.concatenate([o[g * tq:(g + 1) * tq] for g in range(groups)], axis=1)


def _dense_attention(q, k, v, *, kv_heads, groups, dk, dv, scale, rows=512, tk=1280):
    bsz, s_len, _ = q.shape
    sk = k.shape[1]
    tq = _largest_tile(s_len, rows // groups, 8)
    tk = _largest_tile(sk, tk, LANES)
    kern = functools.partial(_dense_attn_kernel, groups=groups, dk=dk, dv=dv, tk=tk, scale=scale)
    return pl.pallas_call(
        kern,
        name="dense_attn",
        grid=(bsz, kv_heads, s_len // tq),
        in_specs=[
            pl.BlockSpec((1, tq, groups * dk), lambda b, h, i: (b, i, h)),
            pl.BlockSpec((1, sk, dk), lambda b, h, i: (b, 0, h)),
            pl.BlockSpec((1, sk, dv), lambda b, h, i: (b, 0, h)),
        ],
        out_specs=pl.BlockSpec((1, tq, groups * dv), lambda b, h, i: (b, i, h)),
        out_shape=jax.ShapeDtypeStruct((bsz, s_len, kv_heads * groups * dv), F32),
        scratch_shapes=[
            pltpu.VMEM((groups * tq, tk), F32),
            pltpu.VMEM((groups * tq, tk), F32),
            pltpu.VMEM((groups * tq, 1), F32),
            pltpu.VMEM((groups * tq, dv + LANES), F32),
        ],
        compiler_params=pltpu.CompilerParams(dimension_semantics=("arbitrary", "arbitrary", "arbitrary"),
                                             vmem_limit_bytes=VMEM_LIMIT_BYTES),
    )(q, k, v)


def _rms_norm(x, g):
    xf = x.astype(F32)
    y = xf * lax.rsqrt(jnp.mean(xf * xf, axis=-1, keepdims=True) + EPS)
    return (y * g.astype(F32)).astype(x.dtype)


def _layer_norm(x, g, b):
    xf = x.astype(F32)
    mu = jnp.mean(xf, axis=-1, keepdims=True)
    var = jnp.mean(jnp.square(xf - mu), axis=-1, keepdims=True)
    y = (xf - mu) * lax.rsqrt(var + EPS) * g.astype(F32) + b.astype(F32)
    return y.astype(x.dtype)


def _modulate(x, shift, scale):
    return x * (1 + scale) + shift


def _axial_rope_tables(rows, rot_dim):
    row = jnp.repeat(jnp.arange(rows, dtype=F32), GRID_W)
    col = jnp.tile(jnp.arange(GRID_W, dtype=F32), rows)
    n_freq = rot_dim // 4
    inv_freq = ROPE_BASE ** (-jnp.arange(n_freq, dtype=F32) / n_freq)
    ang = jnp.concatenate([row[:, None] * inv_freq, col[:, None] * inv_freq], axis=-1)
    return jnp.cos(ang), jnp.sin(ang)


def _apply_rope(x, cos, sin):
    half = x.shape[-1] // 2
    x1, x2 = x[..., :half], x[..., half:]
    c = cos[None, :, None, :].astype(x.dtype)
    s = sin[None, :, None, :].astype(x.dtype)
    return jnp.concatenate([x1 * c - x2 * s, x1 * s + x2 * c], axis=-1)


def _attn_heads(p, n_q, n_kv):
    bsz, t_len, _ = p.shape
    q, k, v = jnp.split(p, [n_q * HEAD_DIM, (n_q + n_kv) * HEAD_DIM], axis=-1)
    return (q.reshape(bsz, t_len, n_q, HEAD_DIM), k.reshape(bsz, t_len, n_kv, HEAD_DIM), v.reshape(bsz, t_len, n_kv, HEAD_DIM))


def _group_q(q, n_kv):
    bsz, t_len, h, d = q.shape
    return q.reshape(bsz, t_len, n_kv, h // n_kv, d)


def _flat(a):
    return a.reshape(a.shape[0], a.shape[1], -1)


def _mla_queries(p, q_norm, w_uq, rope):
    bsz, t_len, _ = p.shape
    q = _linear(_rms_norm(p[..., :MLA_Q_RANK], q_norm), w_uq).reshape(bsz, t_len, MLA_HEADS, MLA_NOPE + MLA_ROPE)
    if rope is not None:
        q = jnp.concatenate([q[..., :MLA_NOPE], _apply_rope(q[..., MLA_NOPE:], *rope)], axis=-1)
    pad = jnp.zeros((bsz, t_len, MLA_HEADS, MLA_QK_PAD - MLA_NOPE - MLA_ROPE), q.dtype)
    return _flat(jnp.concatenate([q, pad], axis=-1))


def _mla_keys_values(p, kv_norm, w_ukv, rope):
    bsz, t_len, _ = p.shape
    ckv = p[..., MLA_Q_RANK:MLA_Q_RANK + MLA_KV_RANK]
    k_rope = p[..., MLA_Q_RANK + MLA_KV_RANK:][:, :, None, :]
    if rope is not None:
        k_rope = _apply_rope(k_rope, *rope)
    kv = _linear(_rms_norm(ckv, kv_norm), w_ukv).reshape(bsz, t_len, MLA_HEADS, MLA_NOPE + MLA_V)
    pad = jnp.zeros((bsz, t_len, MLA_HEADS, MLA_QK_PAD - MLA_NOPE - MLA_ROPE), kv.dtype)
    k = jnp.concatenate([kv[..., :MLA_NOPE], jnp.broadcast_to(k_rope, (bsz, t_len, MLA_HEADS, MLA_ROPE)), pad], axis=-1)
    return _flat(k).astype(BF16), _flat(kv[..., MLA_NOPE:]).astype(BF16)


def _mla_mixer(px, pc, q_norm, w_uq, kv_norm, w_ukv, rope, ctx_out):
    scale = (MLA_NOPE + MLA_ROPE) ** -0.5
    attn = functools.partial(_dense_attention, kv_heads=MLA_HEADS, groups=1, dk=MLA_QK_PAD, dv=MLA_V, scale=scale)
    kx, vx = _mla_keys_values(px, kv_norm, w_ukv, rope)
    kc, vc = _mla_keys_values(pc, kv_norm, w_ukv, None)
    qx = _mla_queries(px, q_norm, w_uq, rope)
    y_x = attn(qx, jnp.concatenate([kc, kx], axis=1), jnp.concatenate([vc, vx], axis=1))
    y_c = attn(_mla_queries(pc, q_norm, w_uq, None), kc, vc) if ctx_out else None
    return y_x, y_c


def _gqa_prep_kernel(p_ref, cos_ref, sin_ref, gq_ref, gk_ref, q_ref, k_ref, v_ref, *, rope):
    x = p_ref[0]

    def prep(col, gain):
        xh = x[:, col * HEAD_DIM:(col + 1) * HEAD_DIM]
        xh = xh * lax.rsqrt(jnp.mean(xh * xh, axis=-1, keepdims=True) + EPS) * gain
        if rope:
            xh = xh * cos_ref[...] + pltpu.roll(xh, HEAD_DIM // 2, 1) * sin_ref[...]
        return xh

    q_ref[0] = jnp.concatenate([prep(h, gq_ref[...]) for h in range(GQA_HEADS)], axis=1)
    k_ref[0] = jnp.concatenate([prep(GQA_HEADS + h, gk_ref[...]) for h in range(GQA_KV_HEADS)], axis=1).astype(BF16)
    v_ref[0] = x[:, (GQA_HEADS + GQA_KV_HEADS) * HEAD_DIM:].astype(BF16)


def _gqa_prep(p, q_norm, k_norm, rope, tm=512):
    bsz, t_len, cols = p.shape
    tm = _largest_tile(t_len, tm, 8)
    if rope is None:
        cos2 = sin2 = jnp.zeros((t_len, HEAD_DIM), F32)
    else:
        cos2 = jnp.concatenate([rope[0], rope[0]], axis=-1)
        sin2 = jnp.concatenate([-rope[1], rope[1]], axis=-1)
    tab = pl.BlockSpec((tm, HEAD_DIM), lambda b, i: (i, 0))
    vec = pl.BlockSpec((1, HEAD_DIM), lambda b, i: (0, 0))
    widths = (GQA_HEADS * HEAD_DIM, GQA_KV_HEADS * HEAD_DIM, GQA_KV_HEADS * HEAD_DIM)
    return pl.pallas_call(
        functools.partial(_gqa_prep_kernel, rope=rope is not None),
        name="gqa_prep",
        grid=(bsz, t_len // tm),
        in_specs=[pl.BlockSpec((1, tm, cols), lambda b, i: (b, i, 0)), tab, tab, vec, vec],
        out_specs=[pl.BlockSpec((1, tm, w), lambda b, i: (b, i, 0)) for w in widths],
        out_shape=[jax.ShapeDtypeStruct((bsz, t_len, w), dt) for w, dt in zip(widths, (F32, BF16, BF16))],
        compiler_params=pltpu.CompilerParams(dimension_semantics=("arbitrary", "arbitrary"),
                                             vmem_limit_bytes=VMEM_LIMIT_BYTES),
    )(p, cos2, sin2, q_norm.reshape(1, HEAD_DIM), k_norm.reshape(1, HEAD_DIM))


def _gqa_mixer(px, pc, q_norm, k_norm, rope, ctx_out):
    scale = HEAD_DIM ** -0.5
    attn = functools.partial(_dense_attention, kv_heads=GQA_KV_HEADS, groups=GQA_HEADS // GQA_KV_HEADS, dk=HEAD_DIM,
                             dv=HEAD_DIM, scale=scale)
    qx, kx, vx = _gqa_prep(px, q_norm, k_norm, rope)
    qc, kc, vc = _gqa_prep(pc, q_norm, k_norm, None)
    y_x = attn(qx, jnp.concatenate([kc, kx], axis=1), jnp.concatenate([vc, vx], axis=1))
    y_c = attn(qc, kc, vc) if ctx_out else None
    return y_x, y_c


def _window_attention(q, k, v, kc, vc, sink, scale):
    bsz, s_len, hk, g, d = q.shape
    nb = s_len // WINDOW
    qb = q.reshape(bsz, nb, WINDOW, hk, g, d)
    pad = ((0, 0), (WINDOW, WINDOW), (0, 0), (0, 0))
    kp = jnp.pad(k, pad).reshape(bsz, nb + 2, WINDOW, hk, d)
    vp = jnp.pad(v, pad).reshape(bsz, nb + 2, WINDOW, hk, d)
    kband = jnp.concatenate([kp[:, :-2], kp[:, 1:-1], kp[:, 2:]], axis=2)
    vband = jnp.concatenate([vp[:, :-2], vp[:, 1:-1], vp[:, 2:]], axis=2)
    s_win = jnp.einsum('bnqhgd,bnkhd->bnhgqk', qb, kband).astype(F32) * scale
    blk = jnp.arange(nb)[:, None] * WINDOW
    q_pos = blk + jnp.arange(WINDOW)[None, :]
    k_pos = blk - WINDOW + jnp.arange(3 * WINDOW)[None, :]
    valid = (jnp.abs(k_pos[:, None, :] - q_pos[:, :, None]) <= WINDOW) & (k_pos >= 0)[:, None, :] & (k_pos < s_len)[:, None, :]
    s_win = jnp.where(valid[None, :, None, None], s_win, -jnp.inf)
    s_ctx = jnp.einsum('bnqhgd,bchd->bnhgqc', qb, kc).astype(F32) * scale
    s_sink = jnp.broadcast_to(sink.reshape(hk, g)[None, None, :, :, None, None].astype(F32), s_ctx.shape[:-1] + (1,))
    p = jax.nn.softmax(jnp.concatenate([s_ctx, s_win, s_sink], axis=-1), axis=-1).astype(v.dtype)
    n_ctx = kc.shape[1]
    o = jnp.einsum('bnhgqc,bchd->bnqhgd', p[..., :n_ctx], vc) + jnp.einsum('bnhgqk,bnkhd->bnqhgd', p[..., n_ctx:n_ctx + 3 * WINDOW], vband)
    return o.reshape(bsz, s_len, hk * g * d)


def _sink_attention(q, k, v, sink, scale):
    bsz, t_len, hk, g, d = q.shape
    s = jnp.einsum('bqhgd,bkhd->bhgqk', q, k).astype(F32) * scale
    s_sink = jnp.broadcast_to(sink.reshape(hk, g)[None, :, :, None, None].astype(F32), s.shape[:-1] + (1,))
    p = jax.nn.softmax(jnp.concatenate([s, s_sink], axis=-1), axis=-1)[..., :-1].astype(v.dtype)
    return jnp.einsum('bhgqk,bkhd->bqhgd', p, v).reshape(bsz, t_len, hk * g * d)


def _swa_mixer(px, pc, sink, rope, ctx_out):
    scale = HEAD_DIM ** -0.5
    qx, kx, vx = _attn_heads(px, SWA_HEADS, SWA_KV_HEADS)
    qc, kc, vc = _attn_heads(pc, SWA_HEADS, SWA_KV_HEADS)
    qx = _apply_rope(qx, *rope)
    kx = _apply_rope(kx, *rope)
    y_x = _window_attention(_group_q(qx, SWA_KV_HEADS), kx, vx, kc, vc, sink, scale)
    y_c = _sink_attention(_group_q(qc, SWA_KV_HEADS), kc, vc, sink, scale) if ctx_out else None
    return y_x, y_c


def _centred_dwconv(x, w, b):
    y = lax.conv_general_dilated(x, w[:, None, :], window_strides=(1,), padding=((SSD_CONV // 2, SSD_CONV // 2),),
                                 dimension_numbers=('NWC', 'WIO', 'NWC'), feature_group_count=x.shape[-1])
    return y + b


def _ssd_chunked(x, dt, A, Bm, Cm, h0, with_output):
    bsz, t_len = x.shape[:2]
    L, G, Hg, P, N = SSD_CHUNK, SSD_GROUPS, SSD_HEADS // SSD_GROUPS, SSD_HEAD_DIM, SSD_STATE
    nc = t_len // L
    xc = x.astype(F32).reshape(bsz, nc, L, G, Hg, P)
    dtc = dt.astype(F32).reshape(bsz, nc, L, G, Hg)
    Bc = Bm.astype(F32).reshape(bsz, nc, L, G, N)
    Cc = Cm.astype(F32).reshape(bsz, nc, L, G, N)
    a_cum = jnp.cumsum(dtc * A.astype(F32).reshape(G, Hg), axis=2)
    a_end = a_cum[:, :, -1]
    states = jnp.einsum('bclgn,bclgh,bclghp->bcghpn', Bc, jnp.exp(a_end[:, :, None] - a_cum) * dtc, xc)

    def carry_state(h, inp):
        decay, st = inp
        return h * decay[..., None, None] + st, h

    h_final, h_enter = lax.scan(carry_state, h0.astype(F32), (jnp.moveaxis(jnp.exp(a_end), 1, 0), jnp.moveaxis(states, 1, 0)))
    if not with_output:
        return None, h_final
    h_enter = jnp.moveaxis(h_enter, 0, 1)
    causal = jnp.tril(jnp.ones((L, L), dtype=bool))
    seg = a_cum[:, :, :, None] - a_cum[:, :, None]
    decay = jnp.exp(jnp.where(causal[:, :, None, None], seg, -jnp.inf))
    cb = jnp.einsum('bclgn,bcsgn->bclsg', Cc, Bc)
    y_diag = jnp.einsum('bclsgh,bcsghp->bclghp', cb[..., None] * decay * dtc[:, :, None], xc)
    y_off = jnp.einsum('bclgn,bcghpn->bclghp', Cc, h_enter) * jnp.exp(a_cum)[..., None]
    return (y_diag + y_off).reshape(bsz, t_len, SSD_HEADS, P), h_final


def _ssd_mixer(px, pc, conv_w, conv_b, a_log, dt_bias, d_skip, norm_g, ctx_out):
    def prep(p):
        bsz, t_len, _ = p.shape
        z, xbc, dt = jnp.split(p, [SSD_INNER, SSD_INNER + SSD_CONV_DIM], axis=-1)
        xbc = jax.nn.silu(_centred_dwconv(xbc, conv_w, conv_b))
        xs, Bm, Cm = jnp.split(xbc, [SSD_INNER, SSD_INNER + SSD_GROUPS * SSD_STATE], axis=-1)
        xs = xs.reshape(bsz, t_len, SSD_HEADS, SSD_HEAD_DIM)
        Bm = Bm.reshape(bsz, t_len, SSD_GROUPS, SSD_STATE)
        Cm = Cm.reshape(bsz, t_len, SSD_GROUPS, SSD_STATE)
        dt = jax.nn.softplus(dt.reshape(bsz, t_len, 2, SSD_HEADS) + dt_bias)
        return z, xs, Bm, Cm, dt

    def flip(a):
        return jnp.flip(a, axis=1)

    A = -jnp.exp(a_log.astype(F32))
    zx, xx, Bx, Cx, dtx = prep(px)
    zc, xc, Bc, Cc, dtc = prep(pc)
    h0 = jnp.zeros((px.shape[0], SSD_GROUPS, SSD_HEADS // SSD_GROUPS, SSD_HEAD_DIM, SSD_STATE), F32)
    yc_f, hc_f = _ssd_chunked(xc, dtc[:, :, 0], A[0], Bc, Cc, h0, ctx_out)
    yx_f, _ = _ssd_chunked(xx, dtx[:, :, 0], A[0], Bx, Cx, hc_f, True)
    yc_b, hc_b = _ssd_chunked(flip(xc), flip(dtc[:, :, 1]), A[1], flip(Bc), flip(Cc), h0, ctx_out)
    yx_b, _ = _ssd_chunked(flip(xx), flip(dtx[:, :, 1]), A[1], flip(Bx), flip(Cx), hc_b, True)

    def finish(y_f, y_b, xs, z):
        y = (y_f + flip(y_b) + d_skip.astype(F32)[:, None] * xs.astype(F32)).astype(xs.dtype)
        bsz, t_len = xs.shape[:2]
        return _rms_norm(y.reshape(bsz, t_len, SSD_INNER) * jax.nn.silu(z), norm_g)

    y_x = finish(yx_f, yx_b, xx, zx)
    y_c = finish(yc_f, yc_b, xc, zc) if ctx_out else None
    return y_x, y_c


PEER_SEL = PEER_HEADS * PEER_TOPK
PEER_HALF = PEER_DKEY // 2
PEER_ROUTE_TILE = 128
PEER_APPLY_TILE = 128
PEER_GROUP = 8
PEER_SLOTS = 8
PEER_LOOKAHEAD = 6
PEER_CHUNK = 256
PEER_BURSTS = 16


def _top_extract(vals, iota, count, payload=None):
    n_rows = vals.shape[0]
    top_v, top_i = [], []
    for _ in range(count):
        m = jnp.max(vals, axis=0, keepdims=True)
        idx = jnp.min(jnp.where(vals == m, iota, n_rows), axis=0, keepdims=True)
        sel = iota == idx
        top_v.append(m)
        top_i.append(idx if payload is None else jnp.sum(jnp.where(sel, payload, 0), axis=0, keepdims=True))
        vals = jnp.where(sel, -jnp.inf, vals)
    return jnp.concatenate(top_v, axis=0), jnp.concatenate(top_i, axis=0)


def _peer_route_kernel(q_ref, keys_ref, e_ref, g_ref, et_ref, gt_ref):
    t_len = q_ref.shape[0]
    iota_k = lax.broadcasted_iota(jnp.int32, (PEER_KEYS, t_len), 0)
    iota_c = lax.broadcasted_iota(jnp.int32, (PEER_TOPK + (PEER_TOPK // 2) ** 2, t_len), 0)

    def head(h, carry):
        tops = []
        for p in range(2):
            col = pl.multiple_of((2 * h + p) * PEER_HALF, PEER_HALF)
            qs = q_ref[:, pl.ds(col, PEER_HALF)].astype(BF16)
            s_t = lax.dot_general(keys_ref[2 * h + p], qs, (((1,), (1,)), ((), ())), preferred_element_type=F32)
            tops.append(_top_extract(s_t, iota_k, PEER_TOPK))
        (s1, i1), (s2, i2) = tops
        half = PEER_TOPK // 2
        cand = jnp.concatenate([s1[0:1] + s2] + [s1[a:a + 1] + s2[:half] for a in range(1, half)]
                               + [s1[half:] + s2[0:1]], axis=0)
        i1s = i1 * PEER_KEYS
        expert = jnp.concatenate([i1s[0:1] + i2] + [i1s[a:a + 1] + i2[:half] for a in range(1, half)]
                                 + [i1s[half:] + i2[0:1]], axis=0)
        sc, e_h = _top_extract(cand, iota_c, PEER_TOPK, payload=expert)
        p_h = jnp.exp(sc - sc[0:1])
        row = pl.multiple_of(h * PEER_TOPK, PEER_TOPK)
        gt_ref[pl.ds(row, PEER_TOPK), :] = p_h / jnp.sum(p_h, axis=0, keepdims=True)
        et_ref[pl.ds(row, PEER_TOPK), :] = e_h
        return carry

    lax.fori_loop(0, PEER_HEADS, head, 0, unroll=2)
    e_ref[...] = et_ref[...].T
    g_ref[...] = gt_ref[...].T


def _peer_route(q, sub_keys):
    t_len = q.shape[0]
    tile = _largest_tile(t_len, PEER_ROUTE_TILE, LANES)
    keys = sub_keys.reshape(PEER_HEADS * 2, PEER_KEYS, PEER_HALF).astype(BF16)
    return pl.pallas_call(
        _peer_route_kernel,
        name="peer_route",
        grid=(t_len // tile,),
        in_specs=[pl.BlockSpec((tile, PEER_HEADS * PEER_DKEY), lambda i: (i, 0)),
                  pl.BlockSpec((PEER_HEADS * 2, PEER_KEYS, PEER_HALF), lambda i: (0, 0, 0))],
        out_specs=[pl.BlockSpec((tile, PEER_SEL), lambda i: (i, 0)),
                   pl.BlockSpec((tile, PEER_SEL), lambda i: (i, 0))],
        out_shape=[jax.ShapeDtypeStruct((t_len, PEER_SEL), jnp.int32), jax.ShapeDtypeStruct((t_len, PEER_SEL), F32)],
        scratch_shapes=[pltpu.VMEM((PEER_SEL, tile), jnp.int32), pltpu.VMEM((PEER_SEL, tile), F32)],
        compiler_params=pltpu.CompilerParams(dimension_semantics=("arbitrary",), vmem_limit_bytes=VMEM_LIMIT_BYTES),
    )(q, keys)


def _pack_peer_table(u, v):
    ub = lax.bitcast_convert_type(u.astype(jnp.bfloat16), jnp.uint16).astype(jnp.uint32)
    vb = lax.bitcast_convert_type(v.astype(jnp.bfloat16), jnp.uint16).astype(jnp.uint32)
    return (ub << 16) | vb


def _peer_apply_kernel(e_ref, e_next_ref, h_ref, sh_ref, sc_ref, g_ref, gate_ref, lng_ref, lnb_ref, tab_ref, rows_ref,
                       o_ref, buf, sem, *, alpha):
    t_len = h_ref.shape[1]
    n_groups = t_len // PEER_GROUP
    step = pl.program_id(0)
    d = h_ref.shape[2]
    chunk = min(d, PEER_CHUNK)
    n_chunks = d // chunk
    n_bursts = min(PEER_BURSTS, 2 * n_chunks)
    steps_per_burst = 2 * n_chunks // n_bursts
    per_burst = PEER_SEL // n_bursts

    def request(ids_ref, tok, slot, rows):
        for j in rows:
            pltpu.make_async_copy(rows_ref.at[ids_ref[tok, j]], buf.at[slot, pl.ds(j, 1)],
                                  sem.at[slot]).start(priority=j % 2)

    def arrived(slot):
        pltpu.make_async_copy(tab_ref.at[pl.ds(0, PEER_SEL)], buf.at[slot], sem.at[slot]).wait()

    @pl.when(step == 0)
    def _():
        for tok in range(PEER_LOOKAHEAD):
            request(e_ref, tok, tok % PEER_SLOTS, range(PEER_SEL))

    row_id = lax.broadcasted_iota(jnp.int32, (PEER_GROUP, PEER_SEL), 0)
    shift, scale, gate = sh_ref[0], sc_ref[0], gate_ref[0]
    ln_g, ln_b = lng_ref[...], lnb_ref[...]

    def group(base, last):
        h8 = h_ref[0, pl.ds(base, PEER_GROUP), :]
        hm8 = (h8 * (1.0 + scale) + shift).astype(BF16)
        g8 = g_ref[pl.ds(base, PEER_GROUP), :]
        ffn = [jnp.zeros((PEER_GROUP, chunk), F32) for _ in range(n_chunks)]
        for r in range(PEER_GROUP):
            arrived(r)
            ahead = r + PEER_LOOKAHEAD
            if last and ahead >= PEER_GROUP:
                ids_ref, tok = e_next_ref, ahead - PEER_GROUP
            else:
                ids_ref, tok = e_ref, base + ahead

            def interleaved_requests(part):
                if part % steps_per_burst == 0:
                    burst = part // steps_per_burst
                    request(ids_ref, tok, ahead % PEER_SLOTS, range(burst * per_burst, (burst + 1) * per_burst))

            s = jnp.zeros((PEER_GROUP, PEER_SEL), F32)
            for c in range(n_chunks):
                interleaved_requests(c)
                words = buf[r, :, c * chunk:(c + 1) * chunk]
                u_rows = lax.bitcast_convert_type(words & jnp.uint32(0xFFFF0000), F32).astype(BF16)
                s = s + lax.dot_general(hm8[:, c * chunk:(c + 1) * chunk], u_rows, (((1,), (1,)), ((), ())),
                                        preferred_element_type=F32)
            w = jnp.where(row_id == r, g8 * jax.nn.gelu(s), 0.0).astype(BF16)
            for c in range(n_chunks):
                interleaved_requests(n_chunks + c)
                words = buf[r, :, c * chunk:(c + 1) * chunk]
                v_rows = lax.bitcast_convert_type(words << 16, F32).astype(BF16)
                ffn[c] = ffn[c] + jnp.dot(w, v_rows, preferred_element_type=F32)
        y = alpha * h8 + gate * jnp.concatenate(ffn, axis=1)
        mu = jnp.mean(y, axis=-1, keepdims=True)
        var = jnp.mean(jnp.square(y - mu), axis=-1, keepdims=True)
        o_ref[0, pl.ds(base, PEER_GROUP), :] = (y - mu) * lax.rsqrt(var + EPS) * ln_g + ln_b

    def full_group(gi, carry):
        group(pl.multiple_of(gi * PEER_GROUP, PEER_GROUP), False)
        return carry

    lax.fori_loop(0, n_groups - 1, full_group, 0)
    group((n_groups - 1) * PEER_GROUP, True)

    @pl.when(step == pl.num_programs(0) - 1)
    def _():
        for tok in range(PEER_LOOKAHEAD):
            arrived(tok % PEER_SLOTS)


def _peer_block(h, shift, scale, gate, w_q, sub_keys, table, ln_g, ln_b, alpha):
    bsz, t_len, d = h.shape
    q = _linear(h, w_q, shift, scale)
    e, g = _peer_route(q.reshape(bsz * t_len, -1), sub_keys)
    tile = _largest_tile(t_len, PEER_APPLY_TILE, PEER_GROUP)
    n_t = t_len // tile
    n_steps = bsz * n_t
    groups_per_tile = tile // PEER_GROUP
    row_spec = pl.BlockSpec((1, 1, d), lambda i: (i // n_t, 0, 0))
    vec_spec = pl.BlockSpec((1, d), lambda i: (0, 0))
    tok_spec = pl.BlockSpec((1, tile, d), lambda i: (i // n_t, i % n_t, 0))
    return pl.pallas_call(
        functools.partial(_peer_apply_kernel, alpha=alpha),
        name="peer_apply",
        grid=(n_steps,),
        in_specs=[
            pl.BlockSpec((tile, PEER_SEL), lambda i: (i, 0), memory_space=pltpu.SMEM),
            pl.BlockSpec((PEER_GROUP, PEER_SEL), lambda i: (jnp.minimum(i + 1, n_steps - 1) * groups_per_tile, 0),
                         memory_space=pltpu.SMEM),
            tok_spec, row_spec, row_spec,
            pl.BlockSpec((tile, PEER_SEL), lambda i: (i, 0)),
            row_spec, vec_spec, vec_spec,
            pl.BlockSpec(memory_space=pl.ANY),
            pl.BlockSpec(memory_space=pl.ANY),
        ],
        out_specs=tok_spec,
        out_shape=jax.ShapeDtypeStruct((bsz, t_len, d), F32),
        scratch_shapes=[pltpu.VMEM((PEER_SLOTS, PEER_SEL, d), jnp.uint32), pltpu.SemaphoreType.DMA((PEER_SLOTS,))],
        compiler_params=pltpu.CompilerParams(dimension_semantics=("arbitrary",), vmem_limit_bytes=VMEM_LIMIT_BYTES),
    )(e, e, h, shift, scale, g, gate, ln_g.reshape(1, d), ln_b.reshape(1, d), table, table.reshape(-1, 1, d))


def kernel(x, c, ctx, c_ctx, w_ada, b_ada, w_in, mla_q_norm, mla_w_uq, mla_kv_norm, mla_w_ukv, gqa_q_norm, gqa_k_norm, ssd_conv_w, ssd_conv_b, ssd_a_log, ssd_dt_bias, ssd_d, ssd_norm, swa_sink, w_out, ln1_g, ln1_b, peer_w_q, peer_sub_keys, peer_u, peer_v, ln2_g, ln2_b):
    bsz = x.shape[0]
    depth = w_in.shape[0]
    rows = x.shape[1] // GRID_W
    rope_mla = _axial_rope_tables(rows, MLA_ROPE)
    rope_head = _axial_rope_tables(rows, HEAD_DIM)
    alpha = (2.0 * depth) ** 0.25
    silu_c = jax.nn.silu(c)[:, None, :]
    silu_cc = jax.nn.silu(c_ctx)[None, None, :]
    for l in range(depth):
        ctx_out = l < depth - 1
        mx = jnp.split(silu_c @ w_ada[l] + b_ada[l], 6, axis=-1)
        mc = [jnp.broadcast_to(m, (bsz,) + m.shape[1:]) for m in jnp.split(silu_cc @ w_ada[l] + b_ada[l], 6, axis=-1)]
        w_in_l = w_in[l].astype(BF16)
        px = jnp.split(_linear(x, w_in_l, mx[0], mx[1]), GROUP_SPLITS, axis=-1)
        pc = jnp.split(_linear(ctx, w_in_l, mc[0], mc[1]), GROUP_SPLITS, axis=-1)
        w_uq = mla_w_uq[l].astype(BF16)
        w_ukv = mla_w_ukv[l].astype(BF16)
        ya_x, ya_c = _mla_mixer(px[0], pc[0], mla_q_norm[l], w_uq, mla_kv_norm[l], w_ukv, rope_mla, ctx_out)
        yb_x, yb_c = _gqa_mixer(px[1], pc[1], gqa_q_norm[l], gqa_k_norm[l], rope_head, ctx_out)
        yc_x, yc_c = _ssd_mixer(px[2], pc[2], ssd_conv_w[l], ssd_conv_b[l], ssd_a_log[l], ssd_dt_bias[l], ssd_d[l], ssd_norm[l], ctx_out)
        yd_x, yd_c = _swa_mixer(px[3], pc[3], swa_sink[l], rope_head, ctx_out)
        w_out_l = w_out[l].astype(BF16)
        w_pq = peer_w_q[l].astype(BF16)
        table = _pack_peer_table(peer_u[l], peer_v[l])
        x = _linear_residual_ln([ya_x, yb_x, yc_x, yd_x], w_out_l, x, mx[2], ln1_g[l], ln1_b[l], alpha)
        x = _peer_block(x, mx[3], mx[4], mx[5], w_pq, peer_sub_keys[l], table, ln2_g[l], ln2_b[l], alpha)
        if ctx_out:
            ctx = _linear_residual_ln([ya_c, yb_c, yc_c, yd_c], w_out_l, ctx, mc[2], ln1_g[l], ln1_b[l], alpha)
            ctx = _peer_block(ctx, mc[3], mc[4], mc[5], w_pq, peer_sub_keys[l], table, ln2_g[l], ln2_b[l], alpha)
    return x
```

```python
import functools
import math

import jax
import jax.numpy as jnp
from jax import lax
from jax.experimental import pallas as pl
from jax.experimental.pallas import tpu as pltpu

GRID_W = 64
ROPE_BASE = 10000.0
EPS = 1e-6
HEAD_DIM = 128

MLA_HEADS = 4
MLA_Q_RANK = 512
MLA_KV_RANK = 256
MLA_NOPE = 128
MLA_ROPE = 64
MLA_V = 128
MLA_QK_PAD = 256

GQA_HEADS = 4
GQA_KV_HEADS = 2

SSD_HEADS = 8
SSD_HEAD_DIM = 64
SSD_INNER = SSD_HEADS * SSD_HEAD_DIM
SSD_GROUPS = 2
SSD_STATE = 128
SSD_CONV = 5
SSD_CHUNK = 128
SSD_CONV_DIM = SSD_INNER + 2 * SSD_GROUPS * SSD_STATE

SWA_HEADS = 4
SWA_KV_HEADS = 2
WINDOW = 128

PEER_HEADS = 8
PEER_KEYS = 128
PEER_TOPK = 16
PEER_DKEY = 256
PEER_BLOCK = 128

MLA_COLS = MLA_Q_RANK + MLA_KV_RANK + MLA_ROPE
GQA_COLS = (GQA_HEADS + 2 * GQA_KV_HEADS) * HEAD_DIM
SSD_COLS = SSD_INNER + SSD_CONV_DIM + 2 * SSD_HEADS
SWA_COLS = (SWA_HEADS + 2 * SWA_KV_HEADS) * HEAD_DIM
GROUP_SPLITS = [MLA_COLS, MLA_COLS + GQA_COLS, MLA_COLS + GQA_COLS + SSD_COLS]

VMEM_LIMIT_BYTES = 56 * 1024 * 1024
LANES = 128

F32 = jnp.float32
BF16 = jnp.bfloat16


def _largest_tile(n, cap, unit):
    return max(t for t in range(unit, min(n, cap) + 1, unit) if n % t == 0)


def _linear_mod_kernel(x_ref, sh_ref, sc_ref, w_ref, o_ref):
    xm = x_ref[0] * (1.0 + sc_ref[0]) + sh_ref[0]
    o_ref[0] = jnp.dot(xm.astype(BF16), w_ref[...], preferred_element_type=F32)


def _linear_kernel(x_ref, w_ref, o_ref):
    o_ref[0] = jnp.dot(x_ref[0].astype(BF16), w_ref[...], preferred_element_type=F32)


def _linear(x, w, shift=None, scale=None, tm=512):
    bsz, t_len, k_dim = x.shape
    n_dim = w.shape[1]
    n_pad = -(-n_dim // LANES) * LANES
    if n_pad != n_dim:
        w = jnp.pad(w, ((0, 0), (0, n_pad - n_dim)))
    tn = _largest_tile(n_pad, 1024, LANES)
    tm = _largest_tile(t_len, tm, 8)
    grid = (bsz, t_len // tm, n_pad // tn)
    x_spec = pl.BlockSpec((1, tm, k_dim), lambda b, i, j: (b, i, 0))
    w_spec = pl.BlockSpec((k_dim, tn), lambda b, i, j: (0, j))
    o_spec = pl.BlockSpec((1, tm, tn), lambda b, i, j: (b, i, j))
    params = pltpu.CompilerParams(dimension_semantics=("arbitrary", "arbitrary", "arbitrary"),
                                  vmem_limit_bytes=VMEM_LIMIT_BYTES)
    out_shape = jax.ShapeDtypeStruct((bsz, t_len, n_pad), F32)
    if shift is None:
        out = pl.pallas_call(_linear_kernel, grid=grid, in_specs=[x_spec, w_spec], out_specs=o_spec,
                             out_shape=out_shape, compiler_params=params, name="linear")(x, w)
    else:
        m_spec = pl.BlockSpec((1, 1, k_dim), lambda b, i, j: (b, 0, 0))
        out = pl.pallas_call(_linear_mod_kernel, grid=grid, in_specs=[x_spec, m_spec, m_spec, w_spec], out_specs=o_spec,
                             out_shape=out_shape, compiler_params=params, name="linear_mod")(x, shift, scale, w)
    return out if n_pad == n_dim else out[..., :n_dim]


def _linear_residual_ln_kernel(*refs, alpha, widths):
    a_refs, (w_ref, x_ref, gate_ref, g_ref, b_ref, o_ref) = refs[:len(widths)], refs[len(widths):]
    mix, row = 0.0, 0
    for a_ref, width in zip(a_refs, widths):
        mix = mix + jnp.dot(a_ref[0].astype(BF16), w_ref[row:row + width, :], preferred_element_type=F32)
        row += width
    y = alpha * x_ref[0] + gate_ref[0] * mix
    mu = jnp.mean(y, axis=-1, keepdims=True)
    var = jnp.mean(jnp.square(y - mu), axis=-1, keepdims=True)
    o_ref[0] = (y - mu) * lax.rsqrt(var + EPS) * g_ref[...] + b_ref[...]


def _linear_residual_ln(pieces, w, x, gate, ln_g, ln_b, alpha, tm=256):
    bsz, t_len, d = x.shape
    widths = tuple(p.shape[2] for p in pieces)
    assert sum(widths) == w.shape[0]
    tm = _largest_tile(t_len, tm, 8)
    row = pl.BlockSpec((1, tm, d), lambda b, i: (b, i, 0))
    vec = pl.BlockSpec((1, d), lambda b, i: (0, 0))
    return pl.pallas_call(
        functools.partial(_linear_residual_ln_kernel, alpha=alpha, widths=widths),
        name="linear_residual_ln",
        grid=(bsz, t_len // tm),
        in_specs=[pl.BlockSpec((1, tm, width), lambda b, i: (b, i, 0)) for width in widths]
        + [pl.BlockSpec(w.shape, lambda b, i: (0, 0)), row, pl.BlockSpec((1, 1, d), lambda b, i: (b, 0, 0)), vec, vec],
        out_specs=row,
        out_shape=jax.ShapeDtypeStruct((bsz, t_len, d), F32),
        compiler_params=pltpu.CompilerParams(dimension_semantics=("arbitrary", "arbitrary"),
                                             vmem_limit_bytes=VMEM_LIMIT_BYTES),
    )(*pieces, w, x, gate, ln_g.reshape(1, d), ln_b.reshape(1, d))


def _dense_attn_kernel(q_ref, k_ref, v_ref, o_ref, s_even, s_odd, m_ref, acc_ref, *, groups, dk, dv, tk, scale):
    tq = q_ref.shape[1]
    n_chunks = k_ref.shape[1] // tk
    q = q_ref[0] * scale
    q = jnp.concatenate([q[:, g * dk:(g + 1) * dk] for g in range(groups)], axis=0).astype(BF16)
    m_ref[...] = jnp.full(m_ref.shape, -jnp.inf, F32)
    acc_ref[...] = jnp.zeros(acc_ref.shape, F32)
    ones_column = (lax.broadcasted_iota(jnp.int32, (tk, LANES), 1) == 0).astype(BF16)

    def scores(j, s_ref):
        kb = k_ref[0, pl.ds(pl.multiple_of(j * tk, tk), tk), :]
        s_ref[...] = lax.dot_general(q, kb, (((1,), (1,)), ((), ())), preferred_element_type=F32)

    def accumulate(j, s_ref):
        vb = jnp.concatenate([v_ref[0, pl.ds(pl.multiple_of(j * tk, tk), tk), :], ones_column], axis=1)
        s = s_ref[...]
        m_old = m_ref[...]
        m_new = jnp.maximum(m_old, jnp.max(s, axis=-1, keepdims=True))
        p = jnp.exp(s - m_new).astype(BF16)
        acc_ref[...] = jnp.exp(m_old - m_new) * acc_ref[...] + jnp.dot(p, vb, preferred_element_type=F32)
        m_ref[...] = m_new

    n_pairs = (n_chunks - 1) // 2
    scores(0, s_even)

    def pair(i, carry):
        scores(2 * i + 1, s_odd)
        accumulate(2 * i, s_even)
        scores(2 * i + 2, s_even)
        accumulate(2 * i + 1, s_odd)
        return carry

    lax.fori_loop(0, n_pairs, pair, 0)
    if n_chunks % 2 == 0:
        scores(n_chunks - 1, s_odd)
        accumulate(n_chunks - 2, s_even)
        accumulate(n_chunks - 1, s_odd)
    else:
        accumulate(n_chunks - 1, s_even)
    acc = acc_ref[...]
    o = acc[:, :dv] / acc[:, dv:dv + 1]
    o_ref[0] = jnp.concatenate([o[g * tq:(g + 1) * tq] for g in range(groups)], axis=1)


def _dense_attention(q, k, v, *, kv_heads, groups, dk, dv, scale, rows=1024, tk=1280):
    bsz, s_len, _ = q.shape
    sk = k.shape[1]
    tq = _largest_tile(s_len, rows // groups, 8)
    tk = _largest_tile(sk, tk, LANES)
    kern = functools.partial(_dense_attn_kernel, groups=groups, dk=dk, dv=dv, tk=tk, scale=scale)
    return pl.pallas_call(
        kern,
        name="dense_attn",
        grid=(bsz, kv_heads, s_len // tq),
        in_specs=[
            pl.BlockSpec((1, tq, groups * dk), lambda b, h, i: (b, i, h)),
            pl.BlockSpec((1, sk, dk), lambda b, h, i: (b, 0, h)),
            pl.BlockSpec((1, sk, dv), lambda b, h, i: (b, 0, h)),
        ],
        out_specs=pl.BlockSpec((1, tq, groups * dv), lambda b, h, i: (b, i, h)),
        out_shape=jax.ShapeDtypeStruct((bsz, s_len, kv_heads * groups * dv), F32),
        scratch_shapes=[
            pltpu.VMEM((groups * tq, tk), F32),
            pltpu.VMEM((groups * tq, tk), F32),
            pltpu.VMEM((groups * tq, 1), F32),
            pltpu.VMEM((groups * tq, dv + LANES), F32),
        ],
        compiler_params=pltpu.CompilerParams(dimension_semantics=("arbitrary", "arbitrary", "arbitrary"),
                                             vmem_limit_bytes=VMEM_LIMIT_BYTES),
    )(q, k, v)


def _rms_norm(x, g):
    xf = x.astype(F32)
    y = xf * lax.rsqrt(jnp.mean(xf * xf, axis=-1, keepdims=True) + EPS)
    return (y * g.astype(F32)).astype(x.dtype)


def _layer_norm(x, g, b):
    xf = x.astype(F32)
    mu = jnp.mean(xf, axis=-1, keepdims=True)
    var = jnp.mean(jnp.square(xf - mu), axis=-1, keepdims=True)
    y = (xf - mu) * lax.rsqrt(var + EPS) * g.astype(F32) + b.astype(F32)
    return y.astype(x.dtype)


def _modulate(x, shift, scale):
    return x * (1 + scale) + shift


def _axial_rope_tables(rows, rot_dim):
    row = jnp.repeat(jnp.arange(rows, dtype=F32), GRID_W)
    col = jnp.tile(jnp.arange(GRID_W, dtype=F32), rows)
    n_freq = rot_dim // 4
    inv_freq = ROPE_BASE ** (-jnp.arange(n_freq, dtype=F32) / n_freq)
    ang = jnp.concatenate([row[:, None] * inv_freq, col[:, None] * inv_freq], axis=-1)
    return jnp.cos(ang), jnp.sin(ang)


def _apply_rope(x, cos, sin):
    half = x.shape[-1] // 2
    x1, x2 = x[..., :half], x[..., half:]
    c = cos[None, :, None, :].astype(x.dtype)
    s = sin[None, :, None, :].astype(x.dtype)
    return jnp.concatenate([x1 * c - x2 * s, x1 * s + x2 * c], axis=-1)


def _attn_heads(p, n_q, n_kv):
    bsz, t_len, _ = p.shape
    q, k, v = jnp.split(p, [n_q * HEAD_DIM, (n_q + n_kv) * HEAD_DIM], axis=-1)
    return (q.reshape(bsz, t_len, n_q, HEAD_DIM), k.reshape(bsz, t_len, n_kv, HEAD_DIM), v.reshape(bsz, t_len, n_kv, HEAD_DIM))


def _group_q(q, n_kv):
    bsz, t_len, h, d = q.shape
    return q.reshape(bsz, t_len, n_kv, h // n_kv, d)


def _flat(a):
    return a.reshape(a.shape[0], a.shape[1], -1)


def _mla_queries(p, q_norm, w_uq, rope):
    bsz, t_len, _ = p.shape
    q = _linear(_rms_norm(p[..., :MLA_Q_RANK], q_norm), w_uq).reshape(bsz, t_len, MLA_HEADS, MLA_NOPE + MLA_ROPE)
    if rope is not None:
        q = jnp.concatenate([q[..., :MLA_NOPE], _apply_rope(q[..., MLA_NOPE:], *rope)], axis=-1)
    pad = jnp.zeros((bsz, t_len, MLA_HEADS, MLA_QK_PAD - MLA_NOPE - MLA_ROPE), q.dtype)
    return _flat(jnp.concatenate([q, pad], axis=-1))


def _mla_keys_values(p, kv_norm, w_ukv, rope):
    bsz, t_len, _ = p.shape
    ckv = p[..., MLA_Q_RANK:MLA_Q_RANK + MLA_KV_RANK]
    k_rope = p[..., MLA_Q_RANK + MLA_KV_RANK:][:, :, None, :]
    if rope is not None:
        k_rope = _apply_rope(k_rope, *rope)
    kv = _linear(_rms_norm(ckv, kv_norm), w_ukv).reshape(bsz, t_len, MLA_HEADS, MLA_NOPE + MLA_V)
    pad = jnp.zeros((bsz, t_len, MLA_HEADS, MLA_QK_PAD - MLA_NOPE - MLA_ROPE), kv.dtype)
    k = jnp.concatenate([kv[..., :MLA_NOPE], jnp.broadcast_to(k_rope, (bsz, t_len, MLA_HEADS, MLA_ROPE)), pad], axis=-1)
    return _flat(k).astype(BF16), _flat(kv[..., MLA_NOPE:]).astype(BF16)


def _mla_mixer(px, pc, q_norm, w_uq, kv_norm, w_ukv, rope, ctx_out):
    scale = (MLA_NOPE + MLA_ROPE) ** -0.5
    attn = functools.partial(_dense_attention, kv_heads=MLA_HEADS, groups=1, dk=MLA_QK_PAD, dv=MLA_V, scale=scale)
    kx, vx = _mla_keys_values(px, kv_norm, w_ukv, rope)
    kc, vc = _mla_keys_values(pc, kv_norm, w_ukv, None)
    qx = _mla_queries(px, q_norm, w_uq, rope)
    y_x = attn(qx, jnp.concatenate([kc, kx], axis=1), jnp.concatenate([vc, vx], axis=1))
    y_c = attn(_mla_queries(pc, q_norm, w_uq, None), kc, vc) if ctx_out else None
    return y_x, y_c


def _gqa_prep_kernel(p_ref, cos_ref, sin_ref, gq_ref, gk_ref, q_ref, k_ref, v_ref, *, rope):
    x = p_ref[0]

    def prep(col, gain):
        xh = x[:, col * HEAD_DIM:(col + 1) * HEAD_DIM]
        xh = xh * lax.rsqrt(jnp.mean(xh * xh, axis=-1, keepdims=True) + EPS) * gain
        if rope:
            xh = xh * cos_ref[...] + pltpu.roll(xh, HEAD_DIM // 2, 1) * sin_ref[...]
        return xh

    q_ref[0] = jnp.concatenate([prep(h, gq_ref[...]) for h in range(GQA_HEADS)], axis=1)
    k_ref[0] = jnp.concatenate([prep(GQA_HEADS + h, gk_ref[...]) for h in range(GQA_KV_HEADS)], axis=1).astype(BF16)
    v_ref[0] = x[:, (GQA_HEADS + GQA_KV_HEADS) * HEAD_DIM:].astype(BF16)


def _gqa_prep(p, q_norm, k_norm, rope, tm=512):
    bsz, t_len, cols = p.shape
    tm = _largest_tile(t_len, tm, 8)
    if rope is None:
        cos2 = sin2 = jnp.zeros((t_len, HEAD_DIM), F32)
    else:
        cos2 = jnp.concatenate([rope[0], rope[0]], axis=-1)
        sin2 = jnp.concatenate([-rope[1], rope[1]], axis=-1)
    tab = pl.BlockSpec((tm, HEAD_DIM), lambda b, i: (i, 0))
    vec = pl.BlockSpec((1, HEAD_DIM), lambda b, i: (0, 0))
    widths = (GQA_HEADS * HEAD_DIM, GQA_KV_HEADS * HEAD_DIM, GQA_KV_HEADS * HEAD_DIM)
    return pl.pallas_call(
        functools.partial(_gqa_prep_kernel, rope=rope is not None),
        name="gqa_prep",
        grid=(bsz, t_len // tm),
        in_specs=[pl.BlockSpec((1, tm, cols), lambda b, i: (b, i, 0)), tab, tab, vec, vec],
        out_specs=[pl.BlockSpec((1, tm, w), lambda b, i: (b, i, 0)) for w in widths],
        out_shape=[jax.ShapeDtypeStruct((bsz, t_len, w), dt) for w, dt in zip(widths, (F32, BF16, BF16))],
        compiler_params=pltpu.CompilerParams(dimension_semantics=("arbitrary", "arbitrary"),
                                             vmem_limit_bytes=VMEM_LIMIT_BYTES),
    )(p, cos2, sin2, q_norm.reshape(1, HEAD_DIM), k_norm.reshape(1, HEAD_DIM))


def _gqa_mixer(px, pc, q_norm, k_norm, rope, ctx_out):
    scale = HEAD_DIM ** -0.5
    attn = functools.partial(_dense_attention, kv_heads=GQA_KV_HEADS, groups=GQA_HEADS // GQA_KV_HEADS, dk=HEAD_DIM,
                             dv=HEAD_DIM, scale=scale)
    qx, kx, vx = _gqa_prep(px, q_norm, k_norm, rope)
    qc, kc, vc = _gqa_prep(pc, q_norm, k_norm, None)
    y_x = attn(qx, jnp.concatenate([kc, kx], axis=1), jnp.concatenate([vc, vx], axis=1))
    y_c = attn(qc, kc, vc) if ctx_out else None
    return y_x, y_c


def _window_attention(q, k, v, kc, vc, sink, scale):
    bsz, s_len, hk, g, d = q.shape
    nb = s_len // WINDOW
    qb = q.reshape(bsz, nb, WINDOW, hk, g, d)
    pad = ((0, 0), (WINDOW, WINDOW), (0, 0), (0, 0))
    kp = jnp.pad(k, pad).reshape(bsz, nb + 2, WINDOW, hk, d)
    vp = jnp.pad(v, pad).reshape(bsz, nb + 2, WINDOW, hk, d)
    kband = jnp.concatenate([kp[:, :-2], kp[:, 1:-1], kp[:, 2:]], axis=2)
    vband = jnp.concatenate([vp[:, :-2], vp[:, 1:-1], vp[:, 2:]], axis=2)
    s_win = jnp.einsum('bnqhgd,bnkhd->bnhgqk', qb, kband).astype(F32) * scale
    blk = jnp.arange(nb)[:, None] * WINDOW
    q_pos = blk + jnp.arange(WINDOW)[None, :]
    k_pos = blk - WINDOW + jnp.arange(3 * WINDOW)[None, :]
    valid = (jnp.abs(k_pos[:, None, :] - q_pos[:, :, None]) <= WINDOW) & (k_pos >= 0)[:, None, :] & (k_pos < s_len)[:, None, :]
    s_win = jnp.where(valid[None, :, None, None], s_win, -jnp.inf)
    s_ctx = jnp.einsum('bnqhgd,bchd->bnhgqc', qb, kc).astype(F32) * scale
    s_sink = jnp.broadcast_to(sink.reshape(hk, g)[None, None, :, :, None, None].astype(F32), s_ctx.shape[:-1] + (1,))
    p = jax.nn.softmax(jnp.concatenate([s_ctx, s_win, s_sink], axis=-1), axis=-1).astype(v.dtype)
    n_ctx = kc.shape[1]
    o = jnp.einsum('bnhgqc,bchd->bnqhgd', p[..., :n_ctx], vc) + jnp.einsum('bnhgqk,bnkhd->bnqhgd', p[..., n_ctx:n_ctx + 3 * WINDOW], vband)
    return o.reshape(bsz, s_len, hk * g * d)


def _sink_attention(q, k, v, sink, scale):
    bsz, t_len, hk, g, d = q.shape
    s = jnp.einsum('bqhgd,bkhd->bhgqk', q, k).astype(F32) * scale
    s_sink = jnp.broadcast_to(sink.reshape(hk, g)[None, :, :, None, None].astype(F32), s.shape[:-1] + (1,))
    p = jax.nn.softmax(jnp.concatenate([s, s_sink], axis=-1), axis=-1)[..., :-1].astype(v.dtype)
    return jnp.einsum('bhgqk,bkhd->bqhgd', p, v).reshape(bsz, t_len, hk * g * d)


def _swa_mixer(px, pc, sink, rope, ctx_out):
    scale = HEAD_DIM ** -0.5
    qx, kx, vx = _attn_heads(px, SWA_HEADS, SWA_KV_HEADS)
    qc, kc, vc = _attn_heads(pc, SWA_HEADS, SWA_KV_HEADS)
    qx = _apply_rope(qx, *rope)
    kx = _apply_rope(kx, *rope)
    y_x = _window_attention(_group_q(qx, SWA_KV_HEADS), kx, vx, kc, vc, sink, scale)
    y_c = _sink_attention(_group_q(qc, SWA_KV_HEADS), kc, vc, sink, scale) if ctx_out else None
    return y_x, y_c


def _centred_dwconv(x, w, b):
    y = lax.conv_general_dilated(x, w[:, None, :], window_strides=(1,), padding=((SSD_CONV // 2, SSD_CONV // 2),),
                                 dimension_numbers=('NWC', 'WIO', 'NWC'), feature_group_count=x.shape[-1])
    return y + b


def _ssd_chunked(x, dt, A, Bm, Cm, h0, with_output):
    bsz, t_len = x.shape[:2]
    L, G, Hg, P, N = SSD_CHUNK, SSD_GROUPS, SSD_HEADS // SSD_GROUPS, SSD_HEAD_DIM, SSD_STATE
    nc = t_len // L
    xc = x.astype(F32).reshape(bsz, nc, L, G, Hg, P)
    dtc = dt.astype(F32).reshape(bsz, nc, L, G, Hg)
    Bc = Bm.astype(F32).reshape(bsz, nc, L, G, N)
    Cc = Cm.astype(F32).reshape(bsz, nc, L, G, N)
    a_cum = jnp.cumsum(dtc * A.astype(F32).reshape(G, Hg), axis=2)
    a_end = a_cum[:, :, -1]
    states = jnp.einsum('bclgn,bclgh,bclghp->bcghpn', Bc, jnp.exp(a_end[:, :, None] - a_cum) * dtc, xc)

    def carry_state(h, inp):
        decay, st = inp
        return h * decay[..., None, None] + st, h

    h_final, h_enter = lax.scan(carry_state, h0.astype(F32), (jnp.moveaxis(jnp.exp(a_end), 1, 0), jnp.moveaxis(states, 1, 0)))
    if not with_output:
        return None, h_final
    h_enter = jnp.moveaxis(h_enter, 0, 1)
    causal = jnp.tril(jnp.ones((L, L), dtype=bool))
    seg = a_cum[:, :, :, None] - a_cum[:, :, None]
    decay = jnp.exp(jnp.where(causal[:, :, None, None], seg, -jnp.inf))
    cb = jnp.einsum('bclgn,bcsgn->bclsg', Cc, Bc)
    y_diag = jnp.einsum('bclsgh,bcsghp->bclghp', cb[..., None] * decay * dtc[:, :, None], xc)
    y_off = jnp.einsum('bclgn,bcghpn->bclghp', Cc, h_enter) * jnp.exp(a_cum)[..., None]
    return (y_diag + y_off).reshape(bsz, t_len, SSD_HEADS, P), h_final


def _ssd_mixer(px, pc, conv_w, conv_b, a_log, dt_bias, d_skip, norm_g, ctx_out):
    def prep(p):
        bsz, t_len, _ = p.shape
        z, xbc, dt = jnp.split(p, [SSD_INNER, SSD_INNER + SSD_CONV_DIM], axis=-1)
        xbc = jax.nn.silu(_centred_dwconv(xbc, conv_w, conv_b))
        xs, Bm, Cm = jnp.split(xbc, [SSD_INNER, SSD_INNER + SSD_GROUPS * SSD_STATE], axis=-1)
        xs = xs.reshape(bsz, t_len, SSD_HEADS, SSD_HEAD_DIM)
        Bm = Bm.reshape(bsz, t_len, SSD_GROUPS, SSD_STATE)
        Cm = Cm.reshape(bsz, t_len, SSD_GROUPS, SSD_STATE)
        dt = jax.nn.softplus(dt.reshape(bsz, t_len, 2, SSD_HEADS) + dt_bias)
        return z, xs, Bm, Cm, dt

    def flip(a):
        return jnp.flip(a, axis=1)

    A = -jnp.exp(a_log.astype(F32))
    zx, xx, Bx, Cx, dtx = prep(px)
    zc, xc, Bc, Cc, dtc = prep(pc)
    h0 = jnp.zeros((px.shape[0], SSD_GROUPS, SSD_HEADS // SSD_GROUPS, SSD_HEAD_DIM, SSD_STATE), F32)
    yc_f, hc_f = _ssd_chunked(xc, dtc[:, :, 0], A[0], Bc, Cc, h0, ctx_out)
    yx_f, _ = _ssd_chunked(xx, dtx[:, :, 0], A[0], Bx, Cx, hc_f, True)
    yc_b, hc_b = _ssd_chunked(flip(xc), flip(dtc[:, :, 1]), A[1], flip(Bc), flip(Cc), h0, ctx_out)
    yx_b, _ = _ssd_chunked(flip(xx), flip(dtx[:, :, 1]), A[1], flip(Bx), flip(Cx), hc_b, True)

    def finish(y_f, y_b, xs, z):
        y = (y_f + flip(y_b) + d_skip.astype(F32)[:, None] * xs.astype(F32)).astype(xs.dtype)
        bsz, t_len = xs.shape[:2]
        return _rms_norm(y.reshape(bsz, t_len, SSD_INNER) * jax.nn.silu(z), norm_g)

    y_x = finish(yx_f, yx_b, xx, zx)
    y_c = finish(yc_f, yc_b, xc, zc) if ctx_out else None
    return y_x, y_c


PEER_SEL = PEER_HEADS * PEER_TOPK
PEER_HALF = PEER_DKEY // 2
PEER_ROUTE_TILE = 128
PEER_APPLY_TILE = 128
PEER_GROUP = 8
PEER_SLOTS = 8
PEER_LOOKAHEAD = 6
PEER_CHUNK = 256
PEER_BURSTS = 16


def _top_extract(vals, iota, count, payload=None):
    n_rows = vals.shape[0]
    top_v, top_i = [], []
    for _ in range(count):
        m = jnp.max(vals, axis=0, keepdims=True)
        idx = jnp.min(jnp.where(vals == m, iota, n_rows), axis=0, keepdims=True)
        sel = iota == idx
        top_v.append(m)
        top_i.append(idx if payload is None else jnp.sum(jnp.where(sel, payload, 0), axis=0, keepdims=True))
        vals = jnp.where(sel, -jnp.inf, vals)
    return jnp.concatenate(top_v, axis=0), jnp.concatenate(top_i, axis=0)


def _peer_route_kernel(q_ref, keys_ref, e_ref, g_ref, et_ref, gt_ref):
    t_len = q_ref.shape[0]
    iota_k = lax.broadcasted_iota(jnp.int32, (PEER_KEYS, t_len), 0)
    iota_c = lax.broadcasted_iota(jnp.int32, (PEER_TOPK + (PEER_TOPK // 2) ** 2, t_len), 0)

    def head(h, carry):
        tops = []
        for p in range(2):
            col = pl.multiple_of((2 * h + p) * PEER_HALF, PEER_HALF)
            qs = q_ref[:, pl.ds(col, PEER_HALF)].astype(BF16)
            s_t = lax.dot_general(keys_ref[2 * h + p], qs, (((1,), (1,)), ((), ())), preferred_element_type=F32)
            tops.append(_top_extract(s_t, iota_k, PEER_TOPK))
        (s1, i1), (s2, i2) = tops
        half = PEER_TOPK // 2
        cand = jnp.concatenate([s1[0:1] + s2] + [s1[a:a + 1] + s2[:half] for a in range(1, half)]
                               + [s1[half:] + s2[0:1]], axis=0)
        i1s = i1 * PEER_KEYS
        expert = jnp.concatenate([i1s[0:1] + i2] + [i1s[a:a + 1] + i2[:half] for a in range(1, half)]
                                 + [i1s[half:] + i2[0:1]], axis=0)
        sc, e_h = _top_extract(cand, iota_c, PEER_TOPK, payload=expert)
        p_h = jnp.exp(sc - sc[0:1])
        row = pl.multiple_of(h * PEER_TOPK, PEER_TOPK)
        gt_ref[pl.ds(row, PEER_TOPK), :] = p_h / jnp.sum(p_h, axis=0, keepdims=True)
        et_ref[pl.ds(row, PEER_TOPK), :] = e_h
        return carry

    lax.fori_loop(0, PEER_HEADS, head, 0, unroll=4)
    e_ref[...] = et_ref[...].T
    g_ref[...] = gt_ref[...].T


def _peer_route(q, sub_keys):
    t_len = q.shape[0]
    tile = _largest_tile(t_len, PEER_ROUTE_TILE, LANES)
    keys = sub_keys.reshape(PEER_HEADS * 2, PEER_KEYS, PEER_HALF).astype(BF16)
    return pl.pallas_call(
        _peer_route_kernel,
        name="peer_route",
        grid=(t_len // tile,),
        in_specs=[pl.BlockSpec((tile, PEER_HEADS * PEER_DKEY), lambda i: (i, 0)),
                  pl.BlockSpec((PEER_HEADS * 2, PEER_KEYS, PEER_HALF), lambda i: (0, 0, 0))],
        out_specs=[pl.BlockSpec((tile, PEER_SEL), lambda i: (i, 0)),
                   pl.BlockSpec((tile, PEER_SEL), lambda i: (i, 0))],
        out_shape=[jax.ShapeDtypeStruct((t_len, PEER_SEL), jnp.int32), jax.ShapeDtypeStruct((t_len, PEER_SEL), F32)],
        scratch_shapes=[pltpu.VMEM((PEER_SEL, tile), jnp.int32), pltpu.VMEM((PEER_SEL, tile), F32)],
        compiler_params=pltpu.CompilerParams(dimension_semantics=("arbitrary",), vmem_limit_bytes=VMEM_LIMIT_BYTES),
    )(q, keys)


def _pack_peer_table(u, v):
    ub = lax.bitcast_convert_type(u.astype(jnp.bfloat16), jnp.uint16).astype(jnp.uint32)
    vb = lax.bitcast_convert_type(v.astype(jnp.bfloat16), jnp.uint16).astype(jnp.uint32)
    return (ub << 16) | vb


def _peer_apply_kernel(e_ref, e_next_ref, h_ref, sh_ref, sc_ref, g_ref, gate_ref, lng_ref, lnb_ref, tab_ref, rows_ref,
                       o_ref, buf, sem, *, alpha):
    t_len = h_ref.shape[1]
    n_groups = t_len // PEER_GROUP
    step = pl.program_id(0)
    d = h_ref.shape[2]
    chunk = min(d, PEER_CHUNK)
    n_chunks = d // chunk
    n_bursts = min(PEER_BURSTS, 2 * n_chunks)
    steps_per_burst = 2 * n_chunks // n_bursts
    per_burst = PEER_SEL // n_bursts

    def request(ids_ref, tok, slot, rows):
        for j in rows:
            pltpu.make_async_copy(rows_ref.at[ids_ref[tok, j]], buf.at[slot, pl.ds(j, 1)],
                                  sem.at[slot]).start(priority=j % 2)

    def arrived(slot):
        pltpu.make_async_copy(tab_ref.at[pl.ds(0, PEER_SEL)], buf.at[slot], sem.at[slot]).wait()

    @pl.when(step == 0)
    def _():
        for tok in range(PEER_LOOKAHEAD):
            request(e_ref, tok, tok % PEER_SLOTS, range(PEER_SEL))

    row_id = lax.broadcasted_iota(jnp.int32, (PEER_GROUP, PEER_SEL), 0)
    shift, scale, gate = sh_ref[0], sc_ref[0], gate_ref[0]
    ln_g, ln_b = lng_ref[...], lnb_ref[...]

    def group(base, last):
        h8 = h_ref[0, pl.ds(base, PEER_GROUP), :]
        hm8 = (h8 * (1.0 + scale) + shift).astype(BF16)
        g8 = g_ref[pl.ds(base, PEER_GROUP), :]
        ffn = [jnp.zeros((PEER_GROUP, chunk), F32) for _ in range(n_chunks)]
        for r in range(PEER_GROUP):
            arrived(r)
            ahead = r + PEER_LOOKAHEAD
            if last and ahead >= PEER_GROUP:
                ids_ref, tok = e_next_ref, ahead - PEER_GROUP
            else:
                ids_ref, tok = e_ref, base + ahead

            def interleaved_requests(part):
                if part % steps_per_burst == 0:
                    burst = part // steps_per_burst
                    request(ids_ref, tok, ahead % PEER_SLOTS, range(burst * per_burst, (burst + 1) * per_burst))

            s = jnp.zeros((PEER_GROUP, PEER_SEL), F32)
            for c in range(n_chunks):
                interleaved_requests(c)
                words = buf[r, :, c * chunk:(c + 1) * chunk]
                u_rows = lax.bitcast_convert_type(words & jnp.uint32(0xFFFF0000), F32).astype(BF16)
                s = s + lax.dot_general(hm8[:, c * chunk:(c + 1) * chunk], u_rows, (((1,), (1,)), ((), ())),
                                        preferred_element_type=F32)
            w = jnp.where(row_id == r, g8 * jax.nn.gelu(s), 0.0).astype(BF16)
            for c in range(n_chunks):
                interleaved_requests(n_chunks + c)
                words = buf[r, :, c * chunk:(c + 1) * chunk]
                v_rows = lax.bitcast_convert_type(words << 16, F32).astype(BF16)
                ffn[c] = ffn[c] + jnp.dot(w, v_rows, preferred_element_type=F32)
        y = alpha * h8 + gate * jnp.concatenate(ffn, axis=1)
        mu = jnp.mean(y, axis=-1, keepdims=True)
        var = jnp.mean(jnp.square(y - mu), axis=-1, keepdims=True)
        o_ref[0, pl.ds(base, PEER_GROUP), :] = (y - mu) * lax.rsqrt(var + EPS) * ln_g + ln_b

    def full_group(gi, carry):
        group(pl.multiple_of(gi * PEER_GROUP, PEER_GROUP), False)
        return carry

    lax.fori_loop(0, n_groups - 1, full_group, 0)
    group((n_groups - 1) * PEER_GROUP, True)

    @pl.when(step == pl.num_programs(0) - 1)
    def _():
        for tok in range(PEER_LOOKAHEAD):
            arrived(tok % PEER_SLOTS)


def _peer_block(h, shift, scale, gate, w_q, sub_keys, table, ln_g, ln_b, alpha):
    bsz, t_len, d = h.shape
    q = _linear(h, w_q, shift, scale)
    e, g = _peer_route(q.reshape(bsz * t_len, -1), sub_keys)
    tile = _largest_tile(t_len, PEER_APPLY_TILE, PEER_GROUP)
    n_t = t_len // tile
    n_steps = bsz * n_t
    groups_per_tile = tile // PEER_GROUP
    row_spec = pl.BlockSpec((1, 1, d), lambda i: (i // n_t, 0, 0))
    vec_spec = pl.BlockSpec((1, d), lambda i: (0, 0))
    tok_spec = pl.BlockSpec((1, tile, d), lambda i: (i // n_t, i % n_t, 0))
    return pl.pallas_call(
        functools.partial(_peer_apply_kernel, alpha=alpha),
        name="peer_apply",
        grid=(n_steps,),
        in_specs=[
            pl.BlockSpec((tile, PEER_SEL), lambda i: (i, 0), memory_space=pltpu.SMEM),
            pl.BlockSpec((PEER_GROUP, PEER_SEL), lambda i: (jnp.minimum(i + 1, n_steps - 1) * groups_per_tile, 0),
                         memory_space=pltpu.SMEM),
            tok_spec, row_spec, row_spec,
            pl.BlockSpec((tile, PEER_SEL), lambda i: (i, 0)),
            row_spec, vec_spec, vec_spec,
            pl.BlockSpec(memory_space=pl.ANY),
            pl.BlockSpec(memory_space=pl.ANY),
        ],
        out_specs=tok_spec,
        out_shape=jax.ShapeDtypeStruct((bsz, t_len, d), F32),
        scratch_shapes=[pltpu.VMEM((PEER_SLOTS, PEER_SEL, d), jnp.uint32), pltpu.SemaphoreType.DMA((PEER_SLOTS,))],
        compiler_params=pltpu.CompilerParams(dimension_semantics=("arbitrary",), vmem_limit_bytes=VMEM_LIMIT_BYTES),
    )(e, e, h, shift, scale, g, gate, ln_g.reshape(1, d), ln_b.reshape(1, d), table, table.reshape(-1, 1, d))


def kernel(x, c, ctx, c_ctx, w_ada, b_ada, w_in, mla_q_norm, mla_w_uq, mla_kv_norm, mla_w_ukv, gqa_q_norm, gqa_k_norm, ssd_conv_w, ssd_conv_b, ssd_a_log, ssd_dt_bias, ssd_d, ssd_norm, swa_sink, w_out, ln1_g, ln1_b, peer_w_q, peer_sub_keys, peer_u, peer_v, ln2_g, ln2_b):
    bsz = x.shape[0]
    depth = w_in.shape[0]
    rows = x.shape[1] // GRID_W
    rope_mla = _axial_rope_tables(rows, MLA_ROPE)
    rope_head = _axial_rope_tables(rows, HEAD_DIM)
    alpha = (2.0 * depth) ** 0.25
    silu_c = jax.nn.silu(c)[:, None, :]
    silu_cc = jax.nn.silu(c_ctx)[None, None, :]
    for l in range(depth):
        ctx_out = l < depth - 1
        mx = jnp.split(silu_c @ w_ada[l] + b_ada[l], 6, axis=-1)
        mc = [jnp.broadcast_to(m, (bsz,) + m.shape[1:]) for m in jnp.split(silu_cc @ w_ada[l] + b_ada[l], 6, axis=-1)]
        w_in_l = w_in[l].astype(BF16)
        px = jnp.split(_linear(x, w_in_l, mx[0], mx[1]), GROUP_SPLITS, axis=-1)
        pc = jnp.split(_linear(ctx, w_in_l, mc[0], mc[1]), GROUP_SPLITS, axis=-1)
        w_uq = mla_w_uq[l].astype(BF16)
        w_ukv = mla_w_ukv[l].astype(BF16)
        ya_x, ya_c = _mla_mixer(px[0], pc[0], mla_q_norm[l], w_uq, mla_kv_norm[l], w_ukv, rope_mla, ctx_out)
        yb_x, yb_c = _gqa_mixer(px[1], pc[1], gqa_q_norm[l], gqa_k_norm[l], rope_head, ctx_out)
        yc_x, yc_c = _ssd_mixer(px[2], pc[2], ssd_conv_w[l], ssd_conv_b[l], ssd_a_log[l], ssd_dt_bias[l], ssd_d[l], ssd_norm[l], ctx_out)
        yd_x, yd_c = _swa_mixer(px[3], pc[3], swa_sink[l], rope_head, ctx_out)
        w_out_l = w_out[l].astype(BF16)
        w_pq = peer_w_q[l].astype(BF16)
        table = _pack_peer_table(peer_u[l], peer_v[l])
        x = _linear_residual_ln([ya_x, yb_x, yc_x, yd_x], w_out_l, x, mx[2], ln1_g[l], ln1_b[l], alpha)
        x = _peer_block(x, mx[3], mx[4], mx[5], w_pq, peer_sub_keys[l], table, ln2_g[l], ln2_b[l], alpha)
        if ctx_out:
            ctx = _linear_residual_ln([ya_c, yb_c, yc_c, yd_c], w_out_l, ctx, mc[2], ln1_g[l], ln1_b[l], alpha)
            ctx = _peer_block(ctx, mc[3], mc[4], mc[5], w_pq, peer_sub_keys[l], table, ln2_g[l], ln2_b[l], alpha)
    return x
```
